```python
import math
import jax, jax.numpy as jnp
from jax import lax
import numpy as np

D_MODEL = 1024
BATCH = 32
SEQ = 2048
DEPTH = 1
DEC_BATCH = 128
DEC_SEQ = 1
PAST_LEN = 8192
PAGE_SIZE = 128

MIX_WIDTH = D_MODEL
A_HEADS = 4
A_QK_DIM = 64
A_HEAD_DIM = 2 * A_QK_DIM
A_WIDTH = A_HEADS * A_HEAD_DIM
R_HEADS = 4
R_KEY_DIM = 128
R_VAL_DIM = (MIX_WIDTH - A_WIDTH) // R_HEADS
R_QK_WIDTH = R_HEADS * R_KEY_DIM
R_WIDTH = R_HEADS * R_VAL_DIM
IN_SPLITS = (A_WIDTH, 2 * A_WIDTH, 3 * A_WIDTH, 3 * A_WIDTH + R_QK_WIDTH,
             3 * A_WIDTH + 2 * R_QK_WIDTH, 3 * A_WIDTH + 2 * R_QK_WIDTH + R_WIDTH)
IN_WIDTH = 3 * A_WIDTH + 2 * R_QK_WIDTH + 2 * R_WIDTH
Q_BLOCK = 128
RET_CHUNK = 128
ALIBI_SLOPES = tuple(2.0 ** (-8.0 * (h + 1) / A_HEADS) for h in range(A_HEADS))
RET_GAMMAS = tuple(1.0 - 2.0 ** (-5.0 - h) for h in range(R_HEADS))
N_EXPERTS = 32
TOP_K = 4
D_FF = D_MODEL
SWIGLU_LIMIT = 7.0
SWIGLU_ALPHA = 1.702
EXPERT_BLOCK = 256
LN_EPS = 1e-5
MASK_VALUE = -1e30
DN_ALPHA = (2.0 * DEPTH) ** 0.25
DN_BETA = (8.0 * DEPTH) ** -0.25

kernel_name = "hymba_diffattn_retnet_moe_deepnorm_adaln_step"


def layer_norm(x, g, b):
    xf = x.astype(jnp.float32)
    mu = xf.mean(-1, keepdims=True)
    var = jnp.square(xf - mu).mean(-1, keepdims=True)
    return ((xf - mu) * lax.rsqrt(var + LN_EPS) * g + b).astype(x.dtype)


def ada_modulation(c, w_ada, b_ada):
    mod = jax.nn.silu(c) @ w_ada + b_ada
    return jnp.split(mod[:, None, :], 6, axis=-1)


def project(u, w_in):
    b, t, _ = u.shape
    aq, ak, av, rq, rk, rv, rg = jnp.split(u @ w_in, IN_SPLITS, axis=-1)
    ha = lambda z: z.reshape(b, t, A_HEADS, -1)
    hr = lambda z: z.reshape(b, t, R_HEADS, -1)
    return ha(aq), ha(ak), ha(av), hr(rq), hr(rk), hr(rv), rg


def diff_lambda(lq1, lk1, lq2, lk2, lam_init):
    f = lambda a, c: jnp.exp(jnp.sum(a.astype(jnp.float32) * c.astype(jnp.float32)))
    return f(lq1, lk1) - f(lq2, lk2) + lam_init


def diff_attn_prompt(q, k, v, lam):
    b, s = q.shape[:2]
    nblk = s // Q_BLOCK
    slopes = jnp.asarray(ALIBI_SLOPES, jnp.float32)[:, None, None, None]
    qf = (q.astype(jnp.float32) * A_QK_DIM ** -0.5).reshape(b, nblk, Q_BLOCK, A_HEADS, 2, A_QK_DIM)
    qf = qf.transpose(1, 0, 2, 3, 4, 5)
    kf = k.astype(jnp.float32).reshape(b, s, A_HEADS, 2, A_QK_DIM)
    vf = v.astype(jnp.float32)
    kpos = jnp.arange(s)

    def block(args):
        qblk, i = args
        qpos = i * Q_BLOCK + jnp.arange(Q_BLOCK)
        sc = jnp.einsum('bqhmd,bkhmd->bhmqk', qblk, kf)
        dist = (qpos[:, None] - kpos[None, :]).astype(jnp.float32)
        sc = jnp.where(dist >= 0, sc - slopes * dist, -jnp.inf)
        p = jax.nn.softmax(sc, axis=-1)
        p = p[:, :, 0] - lam * p[:, :, 1]
        return jnp.einsum('bhqk,bkhd->bqhd', p, vf)

    o = lax.map(block, (qf, jnp.arange(nblk)))
    return o.transpose(1, 0, 2, 3, 4).reshape(b, s, A_HEADS, A_HEAD_DIM)


def diff_attn_sample(q, k, v, cache_k, cache_v, layer, page_table, lam):
    bd, t = q.shape[:2]
    n_pages = page_table.shape[1]
    slopes = jnp.asarray(ALIBI_SLOPES, jnp.float32)[:, None, None, None]
    qf = (q.astype(jnp.float32) * A_QK_DIM ** -0.5).reshape(bd, t, A_HEADS, 2, A_QK_DIM)
    qpos = n_pages * PAGE_SIZE + jnp.arange(t)

    def update(carry, kblk, vblk, kpos):
        m, l, acc = carry
        kb = kblk.astype(jnp.float32).reshape(bd, -1, A_HEADS, 2, A_QK_DIM)
        sc = jnp.einsum('bqhmd,bkhmd->bhmqk', qf, kb)
        dist = (qpos[:, None] - kpos[None, :]).astype(jnp.float32)
        sc = jnp.where(dist >= 0, sc - slopes * dist, -jnp.inf)
        m_new = jnp.maximum(m, sc.max(-1))
        corr = jnp.exp(m - m_new)
        p = jnp.exp(sc - m_new[..., None])
        l = l * corr + p.sum(-1)
        acc = acc * corr[..., None] + jnp.einsum('bhmqk,bkhd->bhmqd', p, vblk.astype(jnp.float32))
        return (m_new, l, acc)

    def page_step(carry, j):
        phys = page_table[:, j]
        kpos = j * PAGE_SIZE + jnp.arange(PAGE_SIZE)
        return update(carry, cache_k[layer, phys], cache_v[layer, phys], kpos), None

    init = (jnp.full((bd, A_HEADS, 2, t), MASK_VALUE, jnp.float32),
            jnp.zeros((bd, A_HEADS, 2, t), jnp.float32),
            jnp.zeros((bd, A_HEADS, 2, t, A_HEAD_DIM), jnp.float32))
    carry, _ = lax.scan(page_step, init, jnp.arange(n_pages))
    m, l, acc = update(carry, k, v, qpos)
    o = acc / l[..., None]
    out = o[:, :, 0] - lam * o[:, :, 1]
    return out.transpose(0, 2, 1, 3)


def ret_log_gamma():
    return jnp.log(jnp.asarray(RET_GAMMAS, jnp.float32))


def retention_chunk(state, q, k, v, lg):
    c = q.shape[2]
    idx = jnp.arange(c, dtype=jnp.float32)
    rel = idx[:, None] - idx[None, :]
    decay = jnp.where(rel >= 0, jnp.exp(lg[:, None, None] * jnp.maximum(rel, 0.0)), 0.0)
    inner = jnp.einsum('bhqd,bhkd->bhqk', q, k) * decay
    o = (jnp.einsum('bhqk,bhkv->bhqv', inner, v)
         + jnp.einsum('bhqd,bhdv->bhqv', q, state) * jnp.exp(lg[:, None] * (idx + 1.0))[..., None])
    k_dec = k * jnp.exp(lg[:, None] * (c - 1.0 - idx))[..., None]
    new_state = state * jnp.exp(lg * c)[:, None, None] + jnp.einsum('bhkd,bhkv->bhdv', k_dec, v)
    return new_state, o


def retention_prompt(q, k, v):
    b, s = q.shape[:2]
    n = s // RET_CHUNK
    lg = ret_log_gamma()
    chunks = lambda z: z.astype(jnp.float32).reshape(b, n, RET_CHUNK, R_HEADS, -1).transpose(1, 0, 3, 2, 4)
    state0 = jnp.zeros((b, R_HEADS, R_KEY_DIM, R_VAL_DIM), jnp.float32)
    state, o = lax.scan(lambda st, xs: retention_chunk(st, xs[0], xs[1], xs[2], lg), state0,
                        (chunks(q), chunks(k) * R_KEY_DIM ** -0.5, chunks(v)))
    return state, o.transpose(1, 0, 3, 2, 4).reshape(b, s, R_HEADS, R_VAL_DIM)


def retention_sample(q, k, v, state):
    to = lambda z: z.astype(jnp.float32).transpose(0, 2, 1, 3)
    new_state, o = retention_chunk(state.astype(jnp.float32), to(q), to(k) * R_KEY_DIM ** -0.5, to(v), ret_log_gamma())
    return new_state, o.transpose(0, 2, 1, 3)


def merge_heads(a_o, r_o, rg, lam_init, a_g, r_g, r_b, w_out):
    b, t = a_o.shape[:2]
    a_n = a_o * lax.rsqrt(jnp.mean(jnp.square(a_o), -1, keepdims=True) + LN_EPS) * a_g * (1.0 - lam_init)
    mu = r_o.mean(-1, keepdims=True)
    var = jnp.square(r_o - mu).mean(-1, keepdims=True)
    r_n = (r_o - mu) * lax.rsqrt(var + LN_EPS) * r_g + r_b
    r_n = jax.nn.silu(rg.astype(jnp.float32)) * r_n.reshape(b, t, R_WIDTH)
    cat = jnp.concatenate([a_n.reshape(b, t, A_WIDTH), r_n], axis=-1).astype(w_out.dtype)
    return cat @ w_out


def clamped_swiglu(gu):
    x_glu = jnp.minimum(gu[..., ::2], SWIGLU_LIMIT)
    x_lin = jnp.clip(gu[..., 1::2], -SWIGLU_LIMIT, SWIGLU_LIMIT)
    return x_glu * jax.nn.sigmoid(SWIGLU_ALPHA * x_glu) * (x_lin + 1.0)


def moe(h, w_router, b_router, w_gu, b_gu, w_down, b_down):
    shp = h.shape
    x = h.reshape(-1, D_MODEL)
    t = x.shape[0]
    logits = (x @ w_router + b_router).astype(jnp.float32)
    top_logit, top_idx = lax.top_k(logits, TOP_K)
    top_w = jax.nn.softmax(top_logit, axis=-1)
    a = t * TOP_K
    flat_e = top_idx.reshape(-1)
    order = jnp.argsort(flat_e)
    sorted_e = flat_e[order]
    sorted_tok = order // TOP_K
    counts = jnp.bincount(flat_e, length=N_EXPERTS)
    padded = (counts + EXPERT_BLOCK - 1) // EXPERT_BLOCK * EXPERT_BLOCK
    pad_end = jnp.cumsum(padded)
    pad_start = pad_end - padded
    start = jnp.cumsum(counts) - counts
    dest = pad_start[sorted_e] + jnp.arange(a) - start[sorted_e]
    n_blocks = -(-a // EXPERT_BLOCK) + N_EXPERTS
    buf = jnp.zeros((n_blocks * EXPERT_BLOCK, D_MODEL), x.dtype).at[dest].set(x[sorted_tok])
    block_e = jnp.minimum(jnp.searchsorted(pad_end, jnp.arange(n_blocks) * EXPERT_BLOCK, side='right'), N_EXPERTS - 1)

    def expert_block(args):
        xb, e = args
        return clamped_swiglu(xb @ w_gu[e] + b_gu[e]) @ w_down[e] + b_down[e]

    ybuf = lax.map(expert_block, (buf.reshape(n_blocks, EXPERT_BLOCK, D_MODEL), block_e)).reshape(-1, D_MODEL)
    contrib = ybuf[dest] * top_w.reshape(-1)[order][:, None].astype(ybuf.dtype)
    return jax.ops.segment_sum(contrib, sorted_tok, num_segments=t).reshape(shp)


def block_tail(x, mix, g1, sh2, sc2, g2, ln1_g, ln1_b, w_router, b_router, w_gu, b_gu, w_down, b_down, ln2_g, ln2_b):
    x = layer_norm(DN_ALPHA * x + (1.0 + g1) * mix, ln1_g, ln1_b)
    f = moe(x * (1.0 + sc2) + sh2, w_router, b_router, w_gu, b_gu, w_down, b_down)
    return layer_norm(DN_ALPHA * x + (1.0 + g2) * f, ln2_g, ln2_b)


def setup_inputs(seed: int = 0) -> dict:
    key = jax.random.key(seed)
    ks = jax.random.split(key, 32)
    n_pages = PAST_LEN // PAGE_SIZE
    n_phys = (5 * DEC_BATCH * n_pages) // 4 + 1
    nrm = lambda k, shape, scale: jax.random.normal(k, shape, jnp.float32) * scale
    col_scale = jnp.concatenate([jnp.ones((2 * A_WIDTH,), jnp.float32), jnp.full((A_WIDTH,), DN_BETA, jnp.float32),
                                 jnp.ones((2 * R_QK_WIDTH,), jnp.float32), jnp.full((R_WIDTH,), DN_BETA, jnp.float32),
                                 jnp.ones((R_WIDTH,), jnp.float32)])
    page_table = jax.random.permutation(ks[7], n_phys)[:DEC_BATCH * n_pages].reshape(DEC_BATCH, n_pages).astype(jnp.int32)
    return {
        "x_prompt": nrm(ks[0], (BATCH, SEQ, D_MODEL), 1.0),
        "x_sample": nrm(ks[1], (DEC_BATCH, DEC_SEQ, D_MODEL), 1.0),
        "c_prompt": nrm(ks[2], (BATCH, D_MODEL), 1.0),
        "c_sample": nrm(ks[3], (DEC_BATCH, D_MODEL), 1.0),
        "cache_k": nrm(ks[4], (DEPTH, n_phys, PAGE_SIZE, A_HEADS, A_HEAD_DIM), 1.0),
        "cache_v": nrm(ks[5], (DEPTH, n_phys, PAGE_SIZE, A_HEADS, A_HEAD_DIM), 1.0),
        "state_ret": nrm(ks[6], (DEPTH, DEC_BATCH, R_HEADS, R_KEY_DIM, R_VAL_DIM), 1.0),
        "page_table": page_table,
        "w_ada": nrm(ks[8], (DEPTH, D_MODEL, 6 * D_MODEL), 0.3 * D_MODEL ** -0.5),
        "b_ada": nrm(ks[9], (DEPTH, 6 * D_MODEL), 0.01),
        "w_in": nrm(ks[10], (DEPTH, D_MODEL, IN_WIDTH), D_MODEL ** -0.5) * col_scale,
        "lambda_q1": nrm(ks[11], (DEPTH, A_QK_DIM), 0.1),
        "lambda_k1": nrm(ks[12], (DEPTH, A_QK_DIM), 0.1),
        "lambda_q2": nrm(ks[13], (DEPTH, A_QK_DIM), 0.1),
        "lambda_k2": nrm(ks[14], (DEPTH, A_QK_DIM), 0.1),
        "a_norm_g": 1.0 + nrm(ks[15], (DEPTH, A_HEAD_DIM), 0.02),
        "r_norm_g": 1.0 + nrm(ks[16], (DEPTH, R_HEADS, R_VAL_DIM), 0.02),
        "r_norm_b": nrm(ks[17], (DEPTH, R_HEADS, R_VAL_DIM), 0.02),
        "w_out": nrm(ks[18], (DEPTH, MIX_WIDTH, D_MODEL), DN_BETA * MIX_WIDTH ** -0.5),
        "ln1_g": 1.0 + nrm(ks[19], (DEPTH, D_MODEL), 0.02),
        "ln1_b": nrm(ks[20], (DEPTH, D_MODEL), 0.02),
        "w_router": nrm(ks[21], (DEPTH, D_MODEL, N_EXPERTS), D_MODEL ** -0.5),
        "b_router": nrm(ks[22], (DEPTH, N_EXPERTS), 0.01),
        "w_gate_up": nrm(ks[23], (DEPTH, N_EXPERTS, D_MODEL, 2 * D_FF), DN_BETA * D_MODEL ** -0.5),
        "b_gate_up": nrm(ks[24], (DEPTH, N_EXPERTS, 2 * D_FF), 0.01),
        "w_down": nrm(ks[25], (DEPTH, N_EXPERTS, D_FF, D_MODEL), DN_BETA * D_FF ** -0.5),
        "b_down": nrm(ks[26], (DEPTH, N_EXPERTS, D_MODEL), 0.01),
        "ln2_g": 1.0 + nrm(ks[27], (DEPTH, D_MODEL), 0.02),
        "ln2_b": nrm(ks[28], (DEPTH, D_MODEL), 0.02),
    }


def reference(x_prompt, x_sample, c_prompt, c_sample, cache_k, cache_v, state_ret, page_table,
              w_ada, b_ada, w_in, lambda_q1, lambda_k1, lambda_q2, lambda_k2, a_norm_g, r_norm_g, r_norm_b,
              w_out, ln1_g, ln1_b, w_router, b_router, w_gate_up, b_gate_up, w_down, b_down, ln2_g, ln2_b):
    xp, xs = x_prompt, x_sample
    kp_l, vp_l, sp_l, ks_l, vs_l, ss_l = [], [], [], [], [], []
    for l in range(DEPTH):
        lam_init = 0.8 - 0.6 * math.exp(-0.3 * l)
        lam = diff_lambda(lambda_q1[l], lambda_k1[l], lambda_q2[l], lambda_k2[l], lam_init)
        tail_w = (ln1_g[l], ln1_b[l], w_router[l], b_router[l], w_gate_up[l], b_gate_up[l],
                  w_down[l], b_down[l], ln2_g[l], ln2_b[l])

        sh1, sc1, g1, sh2, sc2, g2 = ada_modulation(c_prompt, w_ada[l], b_ada[l])
        aq, ak, av, rq, rk, rv, rg = project(xp * (1.0 + sc1) + sh1, w_in[l])
        a_o = diff_attn_prompt(aq, ak, av, lam)
        r_state_p, r_o = retention_prompt(rq, rk, rv)
        mix = merge_heads(a_o, r_o, rg, lam_init, a_norm_g[l], r_norm_g[l], r_norm_b[l], w_out[l])
        xp = block_tail(xp, mix, g1, sh2, sc2, g2, *tail_w)
        kp_l.append(ak)
        vp_l.append(av)
        sp_l.append(r_state_p)

        sh1, sc1, g1, sh2, sc2, g2 = ada_modulation(c_sample, w_ada[l], b_ada[l])
        aq, ak, av, rq, rk, rv, rg = project(xs * (1.0 + sc1) + sh1, w_in[l])
        a_o = diff_attn_sample(aq, ak, av, cache_k, cache_v, l, page_table, lam)
        r_state_s, r_o = retention_sample(rq, rk, rv, state_ret[l])
        mix = merge_heads(a_o, r_o, rg, lam_init, a_norm_g[l], r_norm_g[l], r_norm_b[l], w_out[l])
        xs = block_tail(xs, mix, g1, sh2, sc2, g2, *tail_w)
        ks_l.append(ak)
        vs_l.append(av)
        ss_l.append(r_state_s)

    return (xp, xs, jnp.stack(kp_l), jnp.stack(vp_l), jnp.stack(sp_l), jnp.stack(ks_l), jnp.stack(vs_l), jnp.stack(ss_l))
```

```python
import functools
import math

import jax
import jax.numpy as jnp
from jax import lax
from jax.experimental import pallas as pl
from jax.experimental.pallas import tpu as pltpu

F32 = jnp.float32
BF16 = jnp.bfloat16
I32 = jnp.int32

D_MODEL = 1024
DEPTH = 1
PAGE_SIZE = 128
A_HEADS = 4
A_QK_DIM = 64
A_HEAD_DIM = 128
A_WIDTH = A_HEADS * A_HEAD_DIM
R_HEADS = 4
R_KEY_DIM = 128
R_VAL_DIM = 128
R_WIDTH = R_HEADS * R_VAL_DIM
IN_WIDTH = 3 * A_WIDTH + 2 * R_HEADS * R_KEY_DIM + 2 * R_WIDTH
RET_CHUNK = 128
ALIBI_SLOPES = tuple(2.0 ** (-8.0 * (h + 1) / A_HEADS) for h in range(A_HEADS))
RET_GAMMAS = tuple(1.0 - 2.0 ** (-5.0 - h) for h in range(R_HEADS))
N_EXPERTS = 32
TOP_K = 4
D_FF = D_MODEL
SWIGLU_LIMIT = 7.0
SWIGLU_ALPHA = 1.702
LN_EPS = 1e-5
MASK_VALUE = -1e30
DN_ALPHA = (2.0 * DEPTH) ** 0.25

LANES = 128
BF16_SUBLANES = 16
VMEM_LIMIT = 56 * 1024 * 1024

EXPERT_ROWS = 256
WIN = 64
SORT_CHUNK = 256


def _cparams(sem, vmem=VMEM_LIMIT):
    return pltpu.CompilerParams(dimension_semantics=sem, vmem_limit_bytes=vmem)


def _ada_kernel(c_ref, w_ref, b_ref, o_ref):
    c = c_ref[...]
    s = (c * jax.nn.sigmoid(c)).astype(BF16)
    o_ref[...] = jnp.dot(s, w_ref[...].astype(BF16), preferred_element_type=F32) + b_ref[...]


def ada_modulation(c, w_ada, b_ada):
    rows = c.shape[0]
    n = w_ada.shape[1]
    tn = 1024
    return pl.pallas_call(
        _ada_kernel,
        out_shape=jax.ShapeDtypeStruct((rows, n), F32),
        grid=(n // tn,),
        in_specs=[pl.BlockSpec((rows, D_MODEL), lambda j: (0, 0)),
                  pl.BlockSpec((D_MODEL, tn), lambda j: (0, j)),
                  pl.BlockSpec((1, tn), lambda j: (0, j))],
        out_specs=pl.BlockSpec((rows, tn), lambda j: (0, j)),
        compiler_params=_cparams(("arbitrary",)),
        name="ada_modulation",
    )(c, w_ada, b_ada.reshape(1, n))


def _inproj_kernel(x_ref, mod_ref, w_ref, aq_ref, ak_ref, av_ref, akb_ref, avb_ref,
                   rq_ref, rk_ref, rv_ref, rg_ref):
    x = x_ref[0]
    sh1 = mod_ref[0, 0:1, :]
    sc1 = mod_ref[0, 1:2, :]
    u = (x * (1.0 + sc1) + sh1).astype(BF16)

    def proj(c):
        return jnp.dot(u, w_ref[:, c * A_WIDTH:(c + 1) * A_WIDTH], preferred_element_type=F32)

    aq_ref[0] = (proj(0) * (A_QK_DIM ** -0.5)).astype(BF16)
    k = proj(1)
    ak_ref[0] = k
    akb_ref[0] = k.astype(BF16)
    v = proj(2)
    av_ref[0] = v
    avb_ref[0] = v.astype(BF16)
    rq_ref[0] = proj(3).astype(BF16)
    rk_ref[0] = (proj(4) * (R_KEY_DIM ** -0.5)).astype(BF16)
    rv_ref[0] = proj(5).astype(BF16)
    rg_ref[0] = proj(6).astype(BF16)


def inproj_prompt(x, mod, w_in_bf16):
    b, s, _ = x.shape
    ts = min(512, s)
    blk = lambda: pl.BlockSpec((1, ts, A_WIDTH), lambda i, j: (i, j, 0))
    f32o = jax.ShapeDtypeStruct((b, s, A_WIDTH), F32)
    b16o = jax.ShapeDtypeStruct((b, s, A_WIDTH), BF16)
    return pl.pallas_call(
        _inproj_kernel,
        out_shape=(b16o, f32o, f32o, b16o, b16o, b16o, b16o, b16o, b16o),
        grid=(b, s // ts),
        in_specs=[pl.BlockSpec((1, ts, D_MODEL), lambda i, j: (i, j, 0)),
                  pl.BlockSpec((1, 6, D_MODEL), lambda i, j: (i, 0, 0)),
                  pl.BlockSpec((D_MODEL, IN_WIDTH), lambda i, j: (0, 0))],
        out_specs=tuple(blk() for _ in range(9)),
        compiler_params=_cparams(("arbitrary", "arbitrary")),
        name="inproj_prompt",
    )(x, mod, w_in_bf16)


def _diff_lambda(lam_ref, lam_init):
    lp = lam_ref[...]
    e1 = jnp.exp(jnp.sum(lp[0:1, :] * lp[1:2, :], axis=-1, keepdims=True))
    e2 = jnp.exp(jnp.sum(lp[2:3, :] * lp[3:4, :], axis=-1, keepdims=True))
    return e1 - e2 + lam_init


def _attn_kernel(slopes_ref, q_ref, k_ref, v_ref, lam_ref, o_ref, *, tq, tk, lam_init):
    h = pl.program_id(1)
    i = pl.program_id(2)
    slope = slopes_ref[h]
    q = q_ref[0]
    lane = lax.broadcasted_iota(I32, (tq, A_HEAD_DIM), 1)
    zero = jnp.zeros_like(q)
    q1 = jnp.where(lane < A_QK_DIM, q, zero)
    q2 = jnp.where(lane >= A_QK_DIM, q, zero)
    rel = (lax.broadcasted_iota(I32, (tq, tk), 0) - lax.broadcasted_iota(I32, (tq, tk), 1))
    nt = (((1,), (1,)), ((), ()))

    def body(j, carry):
        m1, l1, a1, m2, l2, a2 = carry
        start = pl.multiple_of(j * tk, tk)
        kb = k_ref[0, pl.ds(start, tk), :]
        vb = v_ref[0, pl.ds(start, tk), :]
        dist = rel + (i * tq - j * tk)
        bias = jnp.where(dist >= 0, -slope * dist.astype(F32), MASK_VALUE)

        def one(qm, m, l, a):
            s = lax.dot_general(qm, kb, nt, preferred_element_type=F32) + bias
            m_new = jnp.maximum(m, jnp.max(s, axis=-1, keepdims=True))
            corr = jnp.exp(m - m_new)
            p = jnp.exp(s - m_new)
            l = l * corr + jnp.sum(p, axis=-1, keepdims=True)
            a = a * corr + jnp.dot(p.astype(BF16), vb, preferred_element_type=F32)
            return m_new, l, a

        m1, l1, a1 = one(q1, m1, l1, a1)
        m2, l2, a2 = one(q2, m2, l2, a2)
        return m1, l1, a1, m2, l2, a2

    mi = jnp.full((tq, 1), MASK_VALUE, F32)
    li = jnp.zeros((tq, 1), F32)
    ai = jnp.zeros((tq, A_HEAD_DIM), F32)
    nkv = (i * tq + tq + tk - 1) // tk
    m1, l1, a1, m2, l2, a2 = lax.fori_loop(0, nkv, body, (mi, li, ai, mi, li, ai))
    lam = _diff_lambda(lam_ref, lam_init)
    o_ref[0] = (a1 / l1 - lam * (a2 / l2)).astype(o_ref.dtype)


def diff_attn_prompt(aq, akb, avb, lam_params, lam_init):
    b, s, _ = aq.shape
    tq = min(512, s)
    tk = tq
    slopes = jnp.asarray(ALIBI_SLOPES, F32)
    grid_spec = pltpu.PrefetchScalarGridSpec(
        num_scalar_prefetch=1,
        grid=(b, A_HEADS, s // tq),
        in_specs=[pl.BlockSpec((1, tq, A_HEAD_DIM), lambda bi, h, i, sl: (bi, i, h)),
                  pl.BlockSpec((1, s, A_HEAD_DIM), lambda bi, h, i, sl: (bi, 0, h)),
                  pl.BlockSpec((1, s, A_HEAD_DIM), lambda bi, h, i, sl: (bi, 0, h)),
                  pl.BlockSpec((4, A_QK_DIM), lambda bi, h, i, sl: (0, 0))],
        out_specs=pl.BlockSpec((1, tq, A_HEAD_DIM), lambda bi, h, i, sl: (bi, i, h)),
    )
    return pl.pallas_call(
        functools.partial(_attn_kernel, tq=tq, tk=tk, lam_init=lam_init),
        out_shape=jax.ShapeDtypeStruct((b, s, A_WIDTH), BF16),
        grid_spec=grid_spec,
        compiler_params=_cparams(("arbitrary", "arbitrary", "arbitrary")),
        name="diff_attn_prompt",
    )(slopes, aq, akb, avb, lam_params)


def _ret_kernel(lg_ref, q_ref, k_ref, v_ref, o_ref, st_ref, *, n_chunks):
    h = pl.program_id(1)
    lg = lg_ref[h]
    c = RET_CHUNK
    row = lax.broadcasted_iota(I32, (c, c), 0)
    col = lax.broadcasted_iota(I32, (c, c), 1)
    rel = (row - col).astype(F32)
    decay = jnp.where(rel >= 0, jnp.exp(lg * jnp.maximum(rel, 0.0)), 0.0)
    rowf = lax.broadcasted_iota(I32, (c, R_VAL_DIM), 0).astype(F32)
    q_dec = jnp.exp(lg * (rowf + 1.0))
    k_dec = jnp.exp(lg * (c - 1.0 - rowf))
    lgc = jnp.full((1, 1), c, F32) * lg
    s_dec = jnp.exp(lgc)
    nt = (((1,), (1,)), ((), ()))
    tn = (((0,), (0,)), ((), ()))
    state = jnp.zeros((R_KEY_DIM, R_VAL_DIM), F32)
    for n in range(n_chunks):
        q = q_ref[0, n * c:(n + 1) * c, :]
        k = k_ref[0, n * c:(n + 1) * c, :]
        v = v_ref[0, n * c:(n + 1) * c, :]
        inner = lax.dot_general(q, k, nt, preferred_element_type=F32) * decay
        o = (jnp.dot(inner.astype(BF16), v, preferred_element_type=F32)
             + jnp.dot(q, state.astype(BF16), preferred_element_type=F32) * q_dec)
        o_ref[0, n * c:(n + 1) * c, :] = o.astype(o_ref.dtype)
        kd = (k.astype(F32) * k_dec).astype(BF16)
        state = state * s_dec + lax.dot_general(kd, v, tn, preferred_element_type=F32)
    st_ref[0, 0] = state


def retention_prompt(rq, rk, rv):
    b, s, _ = rq.shape
    lg = jnp.asarray([math.log(g) for g in RET_GAMMAS], F32)
    blk = lambda: pl.BlockSpec((1, s, R_VAL_DIM), lambda bi, h, lgr: (bi, 0, h))
    grid_spec = pltpu.PrefetchScalarGridSpec(
        num_scalar_prefetch=1,
        grid=(b, R_HEADS),
        in_specs=[blk(), blk(), blk()],
        out_specs=(blk(), pl.BlockSpec((1, 1, R_KEY_DIM, R_VAL_DIM), lambda bi, h, lgr: (bi, h, 0, 0))),
    )
    return pl.pallas_call(
        functools.partial(_ret_kernel, n_chunks=s // RET_CHUNK),
        out_shape=(jax.ShapeDtypeStruct((b, s, R_WIDTH), BF16),
                   jax.ShapeDtypeStruct((b, R_HEADS, R_KEY_DIM, R_VAL_DIM), F32)),
        grid_spec=grid_spec,
        compiler_params=_cparams(("arbitrary", "arbitrary")),
        name="retention_prompt",
    )(lg, rq, rk, rv)


def _merge_kernel(ao_ref, ro_ref, rg_ref, x_ref, mod_ref, ag_ref, rgn_ref, rbn_ref, wout_ref,
                  g1_ref, b1_ref, wr_ref, br_ref,
                  x1_ref, h2_ref, idx_ref, w_ref, rank_ref, cnt_ref, *, ts, lam_init, per_row_mod):
    if per_row_mod:
        ao, ro, rg, x = ao_ref[...], ro_ref[...], rg_ref[...], x_ref[...]
        gate1, shift2, scale2 = mod_ref[:, 2, :], mod_ref[:, 3, :], mod_ref[:, 4, :]
    else:
        ao, ro, rg, x = ao_ref[0], ro_ref[0], rg_ref[0], x_ref[0]
        gate1, shift2, scale2 = mod_ref[0, 2:3, :], mod_ref[0, 3:4, :], mod_ref[0, 4:5, :]
    ao = ao.astype(F32)
    ro = ro.astype(F32)
    rg = rg.astype(F32)
    pieces = []
    for h in range(A_HEADS):
        a = ao[:, h * A_HEAD_DIM:(h + 1) * A_HEAD_DIM]
        ms = jnp.mean(a * a, axis=-1, keepdims=True)
        pieces.append((a * lax.rsqrt(ms + LN_EPS) * ag_ref[...] * (1.0 - lam_init)).astype(BF16))
    for h in range(R_HEADS):
        r = ro[:, h * R_VAL_DIM:(h + 1) * R_VAL_DIM]
        mu = jnp.mean(r, axis=-1, keepdims=True)
        d = r - mu
        var = jnp.mean(d * d, axis=-1, keepdims=True)
        rn = d * lax.rsqrt(var + LN_EPS) * rgn_ref[h:h + 1, :] + rbn_ref[h:h + 1, :]
        g = rg[:, h * R_VAL_DIM:(h + 1) * R_VAL_DIM]
        pieces.append(((g * jax.nn.sigmoid(g)) * rn).astype(BF16))
    cat = jnp.concatenate(pieces, axis=-1)
    mix = jnp.dot(cat, wout_ref[...], preferred_element_type=F32)
    y = DN_ALPHA * x + (1.0 + gate1) * mix
    mu = jnp.mean(y, axis=-1, keepdims=True)
    d = y - mu
    var = jnp.mean(d * d, axis=-1, keepdims=True)
    x1 = d * lax.rsqrt(var + LN_EPS) * g1_ref[...] + b1_ref[...]
    h2 = x1 * (1.0 + scale2) + shift2
    if per_row_mod:
        x1_ref[...] = x1
    else:
        x1_ref[0] = x1
    h2_ref[...] = h2.astype(BF16)

    logits = jnp.dot(h2, wr_ref[...], preferred_element_type=F32,
                     precision=lax.Precision.HIGHEST) + br_ref[...]
    lane = lax.broadcasted_iota(I32, (ts, N_EXPERTS), 1).astype(F32)
    lane4 = lax.broadcasted_iota(I32, (ts, TOP_K), 1)
    vals = logits
    member = jnp.zeros((ts, N_EXPERTS), F32)
    sels, tops = [], []
    idx_out = jnp.zeros((ts, TOP_K), I32)
    for k in range(TOP_K):
        m = jnp.max(vals, axis=-1, keepdims=True)
        first = jnp.min(jnp.where(vals == m, lane, float(N_EXPERTS)), axis=-1, keepdims=True)
        sel = lane == first
        sels.append(sel)
        tops.append(m)
        member = member + sel.astype(F32)
        vals = jnp.where(sel, -jnp.inf, vals)
        idx_out = jnp.where(lane4 == k, first.astype(I32), idx_out)
    es = [jnp.exp(t - tops[0]) for t in tops]
    den = es[0] + es[1] + es[2] + es[3]
    w_out = jnp.zeros((ts, TOP_K), F32)
    for k in range(TOP_K):
        w_out = jnp.where(lane4 == k, es[k] / den, w_out)
    tril = (lax.broadcasted_iota(I32, (ts, ts), 1) < lax.broadcasted_iota(I32, (ts, ts), 0))
    prefix = jnp.dot(tril.astype(BF16), member.astype(BF16), preferred_element_type=F32)
    rank_out = jnp.zeros((ts, TOP_K), I32)
    for k in range(TOP_K):
        rk = jnp.sum(jnp.where(sels[k], prefix, 0.0), axis=-1, keepdims=True)
        rank_out = jnp.where(lane4 == k, rk.astype(I32), rank_out)
    idx_ref[...] = idx_out
    w_ref[...] = w_out
    rank_ref[...] = rank_out
    cnt_ref[0] = jnp.sum(member, axis=0, keepdims=True).astype(I32)


def merge_router(ao, ro, rg, x, mod, p, lam_init, ts, per_row_mod):
    if per_row_mod:
        t = x.shape[0]
        grid = (t // ts,)
        tok = lambda w: pl.BlockSpec((ts, w), lambda j: (j, 0))
        in_tok = [tok(A_WIDTH), tok(R_WIDTH), tok(R_WIDTH), tok(D_MODEL),
                  pl.BlockSpec((ts, 6, D_MODEL), lambda j: (j, 0, 0))]
        x1_spec = tok(D_MODEL)
        x1_shape = jax.ShapeDtypeStruct((t, D_MODEL), F32)
        flat = lambda j: j
        const = lambda shape: pl.BlockSpec(shape, lambda j: (0,) * len(shape))
        sem = ("arbitrary",)
    else:
        b, s, _ = x.shape
        t = b * s
        nts = s // ts
        grid = (b, nts)
        tok = lambda w: pl.BlockSpec((1, ts, w), lambda i, j: (i, j, 0))
        in_tok = [tok(A_WIDTH), tok(R_WIDTH), tok(R_WIDTH), tok(D_MODEL),
                  pl.BlockSpec((1, 6, D_MODEL), lambda i, j: (i, 0, 0))]
        x1_spec = tok(D_MODEL)
        x1_shape = jax.ShapeDtypeStruct((b, s, D_MODEL), F32)
        flat = lambda i, j: i * nts + j
        const = lambda shape: pl.BlockSpec(shape, lambda i, j: (0,) * len(shape))
        sem = ("arbitrary", "arbitrary")
    nt = t // ts
    small = lambda: pl.BlockSpec((ts, TOP_K), lambda *a: (flat(*a), 0))
    return pl.pallas_call(
        functools.partial(_merge_kernel, ts=ts, lam_init=lam_init, per_row_mod=per_row_mod),
        out_shape=(x1_shape,
                   jax.ShapeDtypeStruct((t, D_MODEL), BF16),
                   jax.ShapeDtypeStruct((t, TOP_K), I32),
                   jax.ShapeDtypeStruct((t, TOP_K), F32),
                   jax.ShapeDtypeStruct((t, TOP_K), I32),
                   jax.ShapeDtypeStruct((nt, 1, N_EXPERTS), I32)),
        grid=grid,
        in_specs=in_tok + [const((1, A_HEAD_DIM)), const((R_HEADS, R_VAL_DIM)), const((R_HEADS, R_VAL_DIM)),
                           const((D_MODEL, D_MODEL)), const((1, D_MODEL)), const((1, D_MODEL)),
                           const((D_MODEL, N_EXPERTS)), const((1, N_EXPERTS))],
        out_specs=(x1_spec,
                   pl.BlockSpec((ts, D_MODEL), lambda *a: (flat(*a), 0)),
                   small(), small(), small(),
                   pl.BlockSpec((1, 1, N_EXPERTS), lambda *a: (flat(*a), 0, 0))),
        compiler_params=_cparams(sem),
        name="merge_router_rows" if per_row_mod else "merge_router",
    )(ao, ro, rg, x, mod, p["a_norm_g"], p["r_norm_g"], p["r_norm_b"], p["w_out"],
      p["ln1_g"], p["ln1_b"], p["w_router"], p["b_router"])


def _round_up(x, m):
    return (x + m - 1) // m * m


def _max_windows(ts):
    return N_EXPERTS + (TOP_K * ts + N_EXPERTS * (BF16_SUBLANES - 1) + WIN - 1) // WIN


def _local_rows(ts):
    return _round_up(TOP_K * ts + N_EXPERTS * (WIN - 1), SORT_CHUNK)


def _num_blocks(t, ts):
    nt = t // ts
    rows = TOP_K * t + nt * N_EXPERTS * (BF16_SUBLANES - 1) + N_EXPERTS * (WIN + EXPERT_ROWS - 1)
    return rows // EXPERT_ROWS + 1


def routing_plan(cnt, ts, n_blocks):
    nt = cnt.shape[0]
    maxw = _max_windows(ts)
    c16 = _round_up(cnt, BF16_SUBLANES)
    nwin = (c16 + WIN - 1) // WIN
    l64 = nwin * WIN
    loff = jnp.cumsum(l64, axis=1) - l64
    n_chunk = (jnp.sum(l64, axis=1) + SORT_CHUNK - 1) // SORT_CHUNK
    tot = jnp.sum(c16, axis=0)
    region = _round_up(tot + WIN, EXPERT_ROWS)
    rend = jnp.cumsum(region)
    goff = (rend - region)[None, :] + jnp.cumsum(c16, axis=0) - c16
    n_used = (rend[-1] // EXPERT_ROWS).astype(I32)
    blk = jnp.arange(n_blocks, dtype=I32)
    block_e = jnp.minimum(jnp.sum((rend // EXPERT_ROWS)[None, :] <= blk[:, None], axis=1), N_EXPERTS - 1).astype(I32)
    cw = jnp.cumsum(nwin, axis=1)
    wi = jnp.arange(maxw, dtype=I32)
    we = jnp.minimum(jnp.sum(cw[:, None, :] <= wi[None, :, None], axis=2), N_EXPERTS - 1)
    wk = wi[None, :] - jnp.take_along_axis(cw - nwin, we, axis=1)
    win_src = jnp.take_along_axis(loff, we, axis=1) + wk * WIN
    win_dst = jnp.take_along_axis(goff, we, axis=1) + wk * WIN
    valid = wi[None, :] < cw[:, -1:]
    win_src = jnp.where(valid, win_src, 0).astype(I32).reshape(-1)
    win_dst = jnp.where(valid, win_dst, 0).astype(I32).reshape(-1)
    return dict(n_win=cw[:, -1].astype(I32), win_src=win_src, win_dst=win_dst, n_chunk=n_chunk.astype(I32),
                loff=loff.astype(I32), block_e=block_e, n_used=n_used.reshape(1))


def _wait_windows(n, src, dst, sem):
    def body(i, c):
        pltpu.make_async_copy(src.at[pl.ds(0, WIN)], dst.at[pl.ds(0, WIN)], sem).wait()
        return c
    lax.fori_loop(0, n, body, 0)


def _dispatch_kernel(nwin_ref, src_ref, dst_ref, nchunk_ref,
                     h2_ref, idxt_ref, rankt_ref, loff_ref, xin_ref, xbuf_ref, sbuf, sem, *, ts, maxw):
    del xin_ref
    j = pl.program_id(0)

    @pl.when(j == 0)
    def _():
        sbuf[...] = jnp.zeros_like(sbuf)

    e_iota = lax.broadcasted_iota(I32, (N_EXPERTS, ts), 0)
    loff = loff_ref[0].astype(F32)
    ld = []
    for k in range(TOP_K):
        hit = idxt_ref[k:k + 1, :] == e_iota
        off = jnp.sum(jnp.where(hit, loff, 0.0), axis=0, keepdims=True)
        ld.append(off.astype(I32) + rankt_ref[k:k + 1, :])
    x = h2_ref[...]
    row = lax.broadcasted_iota(I32, (SORT_CHUNK, ts), 0)

    def chunk(c, carry):
        r0 = pl.multiple_of(c * SORT_CHUNK, SORT_CHUNK)
        rows = row + r0
        onehot = (ld[0] == rows) | (ld[1] == rows) | (ld[2] == rows) | (ld[3] == rows)
        p = jnp.where(onehot, 1.0, 0.0).astype(BF16)
        sbuf[pl.ds(r0, SORT_CHUNK), :] = jnp.dot(p, x, preferred_element_type=F32).astype(BF16)
        return carry

    lax.fori_loop(0, nchunk_ref[j], chunk, 0)
    n = nwin_ref[j]

    def issue(i, carry):
        s = pl.multiple_of(src_ref[j * maxw + i], BF16_SUBLANES)
        d = pl.multiple_of(dst_ref[j * maxw + i], BF16_SUBLANES)
        pltpu.make_async_copy(sbuf.at[pl.ds(s, WIN)], xbuf_ref.at[pl.ds(d, WIN)], sem).start()
        return carry

    lax.fori_loop(0, n, issue, 0)
    _wait_windows(n, sbuf, xbuf_ref, sem)


def moe_dispatch(h2, idx_t, rank_t, plan, ts, n_blocks):
    t = h2.shape[0]
    nt = t // ts
    maxw = _max_windows(ts)
    lrows = _local_rows(ts)
    xin = jnp.zeros((n_blocks * EXPERT_ROWS, D_MODEL), BF16)
    grid_spec = pltpu.PrefetchScalarGridSpec(
        num_scalar_prefetch=4,
        grid=(nt,),
        in_specs=[pl.BlockSpec((ts, D_MODEL), lambda j, *_: (j, 0)),
                  pl.BlockSpec((TOP_K, ts), lambda j, *_: (0, j)),
                  pl.BlockSpec((TOP_K, ts), lambda j, *_: (0, j)),
                  pl.BlockSpec((1, N_EXPERTS, 1), lambda j, *_: (j, 0, 0)),
                  pl.BlockSpec(memory_space=pl.ANY)],
        out_specs=pl.BlockSpec(memory_space=pl.ANY),
        scratch_shapes=[pltpu.VMEM((lrows + WIN, D_MODEL), BF16), pltpu.SemaphoreType.DMA],
    )
    return pl.pallas_call(
        functools.partial(_dispatch_kernel, ts=ts, maxw=maxw),
        out_shape=jax.ShapeDtypeStruct((n_blocks * EXPERT_ROWS, D_MODEL), BF16),
        grid_spec=grid_spec,
        input_output_aliases={8: 0},
        compiler_params=_cparams(("arbitrary",)),
        name="moe_dispatch",
    )(plan["n_win"], plan["win_src"], plan["win_dst"], plan["n_chunk"],
      h2, idx_t, rank_t, plan["loff"].reshape(nt, N_EXPERTS, 1), xin)


def _expert_kernel(be_ref, nu_ref, x_ref, wg_ref, bg_ref, wd_ref, bd_ref, y_ref):
    b = pl.program_id(0)

    @pl.when(b < nu_ref[0])
    def _():
        gu = jnp.dot(x_ref[...], wg_ref[0], preferred_element_type=F32) + bg_ref[0]
        glu = jnp.minimum(gu[:, :D_FF], SWIGLU_LIMIT)
        lin = jnp.clip(gu[:, D_FF:], -SWIGLU_LIMIT, SWIGLU_LIMIT)
        act = glu * jax.nn.sigmoid(SWIGLU_ALPHA * glu) * (lin + 1.0)
        y = jnp.dot(act.astype(BF16), wd_ref[0], preferred_element_type=F32) + bd_ref[0]
        y_ref[...] = y.astype(BF16)

    @pl.when(b >= nu_ref[0])
    def _():
        y_ref[...] = jnp.zeros_like(y_ref)


def moe_experts(xbuf, plan, w_gu, b_gu, w_dn, b_dn, n_blocks):
    row_map = lambda b, be, nu: (jnp.minimum(b, nu[0] - 1), 0)
    out_map = lambda b, be, nu: (b, 0)
    e_map3 = lambda b, be, nu: (be[b], 0, 0)
    grid_spec = pltpu.PrefetchScalarGridSpec(
        num_scalar_prefetch=2,
        grid=(n_blocks,),
        in_specs=[pl.BlockSpec((EXPERT_ROWS, D_MODEL), row_map),
                  pl.BlockSpec((1, D_MODEL, 2 * D_FF), e_map3),
                  pl.BlockSpec((1, 1, 2 * D_FF), e_map3),
                  pl.BlockSpec((1, D_FF, D_MODEL), e_map3),
                  pl.BlockSpec((1, 1, D_MODEL), e_map3)],
        out_specs=pl.BlockSpec((EXPERT_ROWS, D_MODEL), out_map),
    )
    return pl.pallas_call(
        _expert_kernel,
        out_shape=jax.ShapeDtypeStruct((n_blocks * EXPERT_ROWS, D_MODEL), BF16),
        grid_spec=grid_spec,
        compiler_params=_cparams(("arbitrary",)),
        name="moe_experts",
    )(plan["block_e"], plan["n_used"], xbuf, w_gu, b_gu, w_dn, b_dn)


def _combine_kernel(nwin_ref, src_ref, dst_ref, nchunk_ref,
                    x1_ref, mod_ref, idx_ref, w_ref, rank_ref, loff_ref, g2_ref, b2_ref, ybuf_ref,
                    o_ref, ywin, sem, *, ts, maxw, per_row_mod):
    j = pl.program_id(0)

    @pl.when(j == 0)
    def _():
        ywin[...] = jnp.zeros_like(ywin)

    n = nwin_ref[j]

    def issue(i, carry):
        s = pl.multiple_of(src_ref[j * maxw + i], BF16_SUBLANES)
        d = pl.multiple_of(dst_ref[j * maxw + i], BF16_SUBLANES)
        pltpu.make_async_copy(ybuf_ref.at[pl.ds(d, WIN)], ywin.at[pl.ds(s, WIN)], sem).start()
        return carry

    lax.fori_loop(0, n, issue, 0)

    e_iota = lax.broadcasted_iota(I32, (ts, N_EXPERTS), 1)
    loff = loff_ref[0].astype(F32)
    ld, wk = [], []
    for k in range(TOP_K):
        hit = idx_ref[:, k:k + 1] == e_iota
        off = jnp.sum(jnp.where(hit, loff, 0.0), axis=-1, keepdims=True)
        ld.append(off.astype(I32) + rank_ref[:, k:k + 1])
        wk.append(w_ref[:, k:k + 1])
    col = lax.broadcasted_iota(I32, (ts, SORT_CHUNK), 1)
    _wait_windows(n, ybuf_ref, ywin, sem)

    def chunk(c, f):
        r0 = pl.multiple_of(c * SORT_CHUNK, SORT_CHUNK)
        cols = col + r0
        pw = jnp.zeros((ts, SORT_CHUNK), F32)
        for k in range(TOP_K):
            pw = pw + jnp.where(ld[k] == cols, wk[k], 0.0)
        return f + jnp.dot(pw.astype(BF16), ywin[pl.ds(r0, SORT_CHUNK), :], preferred_element_type=F32)

    f = lax.fori_loop(0, nchunk_ref[j], chunk, jnp.zeros((ts, D_MODEL), F32))
    if per_row_mod:
        x1 = x1_ref[...]
        gate2 = mod_ref[:, 5, :]
    else:
        x1 = x1_ref[...]
        gate2 = mod_ref[0, 5:6, :]
    y = DN_ALPHA * x1 + (1.0 + gate2) * f
    mu = jnp.mean(y, axis=-1, keepdims=True)
    d = y - mu
    var = jnp.mean(d * d, axis=-1, keepdims=True)
    o_ref[...] = d * lax.rsqrt(var + LN_EPS) * g2_ref[...] + b2_ref[...]


def moe_combine(x1, mod, idx, w, rank, plan, ybuf, ln2_g, ln2_b, ts, tiles_per_mod, per_row_mod):
    t = x1.shape[0]
    nt = t // ts
    maxw = _max_windows(ts)
    lrows = _local_rows(ts)
    if per_row_mod:
        mod_spec = pl.BlockSpec((ts, 6, D_MODEL), lambda j, *_: (j, 0, 0))
    else:
        mod_spec = pl.BlockSpec((1, 6, D_MODEL), lambda j, *_: (j // tiles_per_mod, 0, 0))
    small = lambda: pl.BlockSpec((ts, TOP_K), lambda j, *_: (j, 0))
    grid_spec = pltpu.PrefetchScalarGridSpec(
        num_scalar_prefetch=4,
        grid=(nt,),
        in_specs=[pl.BlockSpec((ts, D_MODEL), lambda j, *_: (j, 0)),
                  mod_spec, small(), small(), small(),
                  pl.BlockSpec((1, 1, N_EXPERTS), lambda j, *_: (j, 0, 0)),
                  pl.BlockSpec((1, D_MODEL), lambda j, *_: (0, 0)),
                  pl.BlockSpec((1, D_MODEL), lambda j, *_: (0, 0)),
                  pl.BlockSpec(memory_space=pl.ANY)],
        out_specs=pl.BlockSpec((ts, D_MODEL), lambda j, *_: (j, 0)),
        scratch_shapes=[pltpu.VMEM((lrows, D_MODEL), BF16), pltpu.SemaphoreType.DMA],
    )
    return pl.pallas_call(
        functools.partial(_combine_kernel, ts=ts, maxw=maxw, per_row_mod=per_row_mod),
        out_shape=jax.ShapeDtypeStruct((t, D_MODEL), F32),
        grid_spec=grid_spec,
        compiler_params=_cparams(("arbitrary",)),
        name="moe_combine_rows" if per_row_mod else "moe_combine",
    )(plan["n_win"], plan["win_src"], plan["win_dst"], plan["n_chunk"],
      x1, mod, idx, w, rank, plan["loff"].reshape(nt, 1, N_EXPERTS), ln2_g, ln2_b, ybuf)


def moe_block(x1_flat, h2, idx, w, rank, cnt, mod, p, ts, tiles_per_mod, per_row_mod):
    t = h2.shape[0]
    n_blocks = _num_blocks(t, ts)
    plan = routing_plan(cnt.reshape(-1, N_EXPERTS), ts, n_blocks)
    xbuf = moe_dispatch(h2, idx.T, rank.T, plan, ts, n_blocks)
    ybuf = moe_experts(xbuf, plan, p["w_gu"], p["b_gu"], p["w_dn"], p["b_dn"], n_blocks)
    return moe_combine(x1_flat, mod, idx, w, rank, plan, ybuf, p["ln2_g"], p["ln2_b"], ts,
                       tiles_per_mod, per_row_mod)


def _inproj_rows_kernel(x_ref, mod_ref, w_ref, h_ref):
    u = (x_ref[...] * (1.0 + mod_ref[:, 1, :]) + mod_ref[:, 0, :]).astype(BF16)
    h_ref[...] = jnp.dot(u, w_ref[...], preferred_element_type=F32)


def inproj_rows(x, mod, w_in_bf16):
    t = x.shape[0]
    tn = A_WIDTH
    return pl.pallas_call(
        _inproj_rows_kernel,
        out_shape=jax.ShapeDtypeStruct((t, IN_WIDTH), F32),
        grid=(IN_WIDTH // tn,),
        in_specs=[pl.BlockSpec((t, D_MODEL), lambda j: (0, 0)),
                  pl.BlockSpec((t, 6, D_MODEL), lambda j: (0, 0, 0)),
                  pl.BlockSpec((D_MODEL, tn), lambda j: (0, j))],
        out_specs=pl.BlockSpec((t, tn), lambda j: (0, j)),
        compiler_params=_cparams(("arbitrary",)),
        name="inproj_rows",
    )(x, mod, w_in_bf16)


def _decode_attn_kernel(pt_ref, q_ref, kn_ref, vn_ref, lam_ref, *rest, pages, n_pages, lam_init):
    del pt_ref
    k_refs = rest[:pages]
    v_refs = rest[pages:2 * pages]
    o_ref = rest[2 * pages]
    m_sc, l_sc, a_sc = rest[2 * pages + 1:]
    s_id = pl.program_id(1)
    n_steps = pl.num_programs(1)
    rows = PAGE_SIZE * A_HEADS
    r8 = 2 * A_HEADS

    @pl.when(s_id == 0)
    def _():
        m_sc[...] = jnp.full_like(m_sc, MASK_VALUE)
        l_sc[...] = jnp.zeros_like(l_sc)
        a_sc[...] = jnp.zeros_like(a_sc)

    q4 = q_ref[0] * (A_QK_DIM ** -0.5)
    q8 = jnp.concatenate([q4, q4], axis=0)
    rlane = lax.broadcasted_iota(I32, (r8, A_HEAD_DIM), 1)
    rrow = lax.broadcasted_iota(I32, (r8, A_HEAD_DIM), 0)
    keep = ((rrow < A_HEADS) & (rlane < A_QK_DIM)) | ((rrow >= A_HEADS) & (rlane >= A_QK_DIM))
    q8 = jnp.where(keep, q8, 0.0)
    q8b = q8.astype(BF16)

    srow = lax.broadcasted_iota(I32, (r8, rows), 0)
    scol = lax.broadcasted_iota(I32, (r8, rows), 1)
    head_r = srow & (A_HEADS - 1)
    own = (scol & (A_HEADS - 1)) == head_r
    slope = jnp.full((r8, rows), ALIBI_SLOPES[0], F32)
    for hh in range(1, A_HEADS):
        slope = jnp.where(head_r == hh, ALIBI_SLOPES[hh], slope)
    pos = scol >> 2
    q_pos = n_pages * PAGE_SIZE
    nt = (((1,), (1,)), ((), ()))

    scores = []
    for i in range(pages):
        page = s_id * pages + i
        dist = (q_pos - page * PAGE_SIZE - pos).astype(F32)
        kb = k_refs[i][0].astype(BF16)
        s = lax.dot_general(q8b, kb, nt, preferred_element_type=F32)
        scores.append(jnp.where(own, s - slope * dist, MASK_VALUE))
    m_old = m_sc[...]
    m_new = m_old
    for s in scores:
        m_new = jnp.maximum(m_new, jnp.max(s, axis=-1, keepdims=True))
    corr = jnp.exp(m_old - m_new)
    l = l_sc[...] * corr
    acc = a_sc[...] * corr
    for i, s in enumerate(scores):
        p = jnp.exp(s - m_new)
        l = l + jnp.sum(p, axis=-1, keepdims=True)
        acc = acc + jnp.dot(p.astype(BF16), v_refs[i][0].astype(BF16), preferred_element_type=F32)
    m_sc[...] = m_new
    l_sc[...] = l
    a_sc[...] = acc

    @pl.when(s_id == n_steps - 1)
    def _():
        kn = kn_ref[0]
        vn = vn_ref[0]
        k8 = jnp.concatenate([kn, kn], axis=0)
        v8 = jnp.concatenate([vn, vn], axis=0)
        s_new = jnp.sum(q8b.astype(F32) * k8.astype(BF16).astype(F32), axis=-1, keepdims=True)
        m_fin = jnp.maximum(m_new, s_new)
        c2 = jnp.exp(m_new - m_fin)
        p_new = jnp.exp(s_new - m_fin)
        l_fin = l * c2 + p_new
        a_fin = acc * c2 + p_new.astype(BF16).astype(F32) * v8.astype(BF16).astype(F32)
        o = a_fin / l_fin
        lam = _diff_lambda(lam_ref, lam_init)
        o_ref[0] = o[:A_HEADS] - lam * o[A_HEADS:]


def diff_attn_sample(q, k_new, v_new, cache_k, cache_v, page_table, lam_params, lam_init):
    bd, n_pages = page_table.shape
    pages = min(16, n_pages)
    rows = PAGE_SIZE * A_HEADS
    pt = page_table.reshape(-1)

    def page_spec(i):
        return pl.BlockSpec((1, rows, A_HEAD_DIM), lambda b, s, ptr: (ptr[b * n_pages + s * pages + i], 0, 0))

    tok = lambda: pl.BlockSpec((1, A_HEADS, A_HEAD_DIM), lambda b, s, ptr: (b, 0, 0))
    grid_spec = pltpu.PrefetchScalarGridSpec(
        num_scalar_prefetch=1,
        grid=(bd, n_pages // pages),
        in_specs=[tok(), tok(), tok(), pl.BlockSpec((4, A_QK_DIM), lambda b, s, ptr: (0, 0))]
        + [page_spec(i) for i in range(pages)] + [page_spec(i) for i in range(pages)],
        out_specs=tok(),
        scratch_shapes=[pltpu.VMEM((2 * A_HEADS, 1), F32), pltpu.VMEM((2 * A_HEADS, 1), F32),
                        pltpu.VMEM((2 * A_HEADS, A_HEAD_DIM), F32)],
    )
    return pl.pallas_call(
        functools.partial(_decode_attn_kernel, pages=pages, n_pages=n_pages, lam_init=lam_init),
        out_shape=jax.ShapeDtypeStruct((bd, A_HEADS, A_HEAD_DIM), F32),
        grid_spec=grid_spec,
        compiler_params=_cparams(("arbitrary", "arbitrary")),
        name="diff_attn_sample",
    )(pt, q, k_new, v_new, lam_params, *([cache_k] * pages), *([cache_v] * pages))


def _ret_step_kernel(q_ref, k_ref, v_ref, st_ref, o_ref, ns_ref, *, nb):
    tn = (((0,), (0,)), ((), ()))
    for i in range(nb):
        for h in range(R_HEADS):
            gamma = RET_GAMMAS[h]
            lgh = math.log(gamma)
            g1 = jnp.exp(jnp.full((1, 1), lgh, F32))
            q = q_ref[i, h:h + 1, :]
            k = k_ref[i, h:h + 1, :] * (R_KEY_DIM ** -0.5)
            v = v_ref[i, h:h + 1, :]
            st = st_ref[i, h]
            qb, kb, vb = q.astype(BF16), k.astype(BF16), v.astype(BF16)
            inner = jnp.sum(qb.astype(F32) * kb.astype(F32), axis=-1, keepdims=True)
            o = (inner.astype(BF16).astype(F32) * vb.astype(F32)
                 + jnp.dot(qb, st.astype(BF16), preferred_element_type=F32) * g1)
            o_ref[i, h:h + 1, :] = o
            k8 = jnp.concatenate([kb, jnp.zeros((7, R_KEY_DIM), BF16)], axis=0)
            v8 = jnp.concatenate([vb, jnp.zeros((7, R_VAL_DIM), BF16)], axis=0)
            ns_ref[i, h] = st * g1 + lax.dot_general(k8, v8, tn, preferred_element_type=F32)


def retention_step(rq, rk, rv, state):
    bd = rq.shape[0]
    nb = 8 if bd % 8 == 0 else 1
    tok = lambda: pl.BlockSpec((nb, R_HEADS, R_VAL_DIM), lambda b: (b, 0, 0))
    st = lambda: pl.BlockSpec((nb, R_HEADS, R_KEY_DIM, R_VAL_DIM), lambda b: (b, 0, 0, 0))
    return pl.pallas_call(
        functools.partial(_ret_step_kernel, nb=nb),
        out_shape=(jax.ShapeDtypeStruct((bd, R_HEADS, R_VAL_DIM), F32),
                   jax.ShapeDtypeStruct((bd, R_HEADS, R_KEY_DIM, R_VAL_DIM), F32)),
        grid=(bd // nb,),
        in_specs=[tok(), tok(), tok(), st()],
        out_specs=(tok(), st()),
        compiler_params=_cparams(("arbitrary",)),
        name="retention_step",
    )(rq, rk, rv, state)


def _prep_params(l, w_in, lambda_q1, lambda_k1, lambda_q2, lambda_k2, a_norm_g, r_norm_g, r_norm_b, w_out,
                 ln1_g, ln1_b, w_router, b_router, w_gate_up, b_gate_up, w_down, b_down, ln2_g, ln2_b):
    wgu = w_gate_up[l]
    bgu = b_gate_up[l]
    return dict(
        w_in=w_in[l].astype(BF16),
        lam=jnp.stack([lambda_q1[l], lambda_k1[l], lambda_q2[l], lambda_k2[l]]),
        a_norm_g=a_norm_g[l].reshape(1, A_HEAD_DIM), r_norm_g=r_norm_g[l], r_norm_b=r_norm_b[l],
        w_out=w_out[l].astype(BF16),
        ln1_g=ln1_g[l].reshape(1, D_MODEL), ln1_b=ln1_b[l].reshape(1, D_MODEL),
        w_router=w_router[l], b_router=b_router[l].reshape(1, N_EXPERTS),
        w_gu=jnp.concatenate([wgu[..., 0::2], wgu[..., 1::2]], axis=-1).astype(BF16),
        b_gu=jnp.concatenate([bgu[..., 0::2], bgu[..., 1::2]], axis=-1).reshape(N_EXPERTS, 1, 2 * D_FF),
        w_dn=w_down[l].astype(BF16), b_dn=b_down[l].reshape(N_EXPERTS, 1, D_MODEL),
        ln2_g=ln2_g[l].reshape(1, D_MODEL), ln2_b=ln2_b[l].reshape(1, D_MODEL),
    )


def kernel(x_prompt, x_sample, c_prompt, c_sample, cache_k, cache_v, state_ret, page_table, w_ada, b_ada, w_in, lambda_q1, lambda_k1, lambda_q2, lambda_k2, a_norm_g, r_norm_g, r_norm_b, w_out, ln1_g, ln1_b, w_router, b_router, w_gate_up, b_gate_up, w_down, b_down, ln2_g, ln2_b):
    b, s, _ = x_prompt.shape
    bd = x_sample.shape[0]
    n_phys = cache_k.shape[1]
    xp = x_prompt
    xs = x_sample.reshape(bd, D_MODEL)
    outs = [[] for _ in range(6)]
    for l in range(DEPTH):
        lam_init = 0.8 - 0.6 * math.exp(-0.3 * l)
        p = _prep_params(l, w_in, lambda_q1, lambda_k1, lambda_q2, lambda_k2, a_norm_g, r_norm_g, r_norm_b,
                         w_out, ln1_g, ln1_b, w_router, b_router, w_gate_up, b_gate_up, w_down, b_down,
                         ln2_g, ln2_b)
        mod = ada_modulation(jnp.concatenate([c_prompt, c_sample], axis=0), w_ada[l], b_ada[l])
        mod = mod.reshape(b + bd, 6, D_MODEL)
        mod_p, mod_s = mod[:b], mod[b:]

        aq, ak, av, akb, avb, rq, rk, rv, rg = inproj_prompt(xp, mod_p, p["w_in"])
        a_o = diff_attn_prompt(aq, akb, avb, p["lam"], lam_init)
        r_o, st_p = retention_prompt(rq, rk, rv)
        ts = min(256, s)
        x1, h2, idx, w, rank, cnt = merge_router(a_o, r_o, rg, xp, mod_p, p, lam_init, ts, False)
        xp = moe_block(x1.reshape(b * s, D_MODEL), h2, idx, w, rank, cnt, mod_p, p, ts, s // ts,
                       False).reshape(b, s, D_MODEL)
        outs[0].append(ak.reshape(b, s, A_HEADS, A_HEAD_DIM))
        outs[1].append(av.reshape(b, s, A_HEADS, A_HEAD_DIM))
        outs[2].append(st_p)

        hs = inproj_rows(xs, mod_s, p["w_in"])
        sp = [hs[:, c * A_WIDTH:(c + 1) * A_WIDTH] for c in range(7)]
        heads = lambda z: z.reshape(bd, A_HEADS, A_HEAD_DIM)
        ck = cache_k[l].reshape(n_phys, PAGE_SIZE * A_HEADS, A_HEAD_DIM)
        cv = cache_v[l].reshape(n_phys, PAGE_SIZE * A_HEADS, A_HEAD_DIM)
        a_s = diff_attn_sample(heads(sp[0]), heads(sp[1]), heads(sp[2]), ck, cv, page_table, p["lam"], lam_init)
        r_s, st_s = retention_step(heads(sp[3]), heads(sp[4]), heads(sp[5]), state_ret[l])
        x1s, h2s, idxs, ws, ranks, cnts = merge_router(
            a_s.reshape(bd, A_WIDTH), r_s.reshape(bd, R_WIDTH), sp[6], xs, mod_s, p, lam_init, bd, True)
        xs = moe_block(x1s, h2s, idxs, ws, ranks, cnts, mod_s, p, bd, 1, True)
        outs[3].append(sp[1].reshape(bd, 1, A_HEADS, A_HEAD_DIM))
        outs[4].append(sp[2].reshape(bd, 1, A_HEADS, A_HEAD_DIM))
        outs[5].append(st_s)

    return (xp, xs.reshape(bd, 1, D_MODEL), jnp.stack(outs[0]), jnp.stack(outs[1]), jnp.stack(outs[2]),
            jnp.stack(outs[3]), jnp.stack(outs[4]), jnp.stack(outs[5]))
```

```python
import functools
import math

import jax
import jax.numpy as jnp
from jax import lax
from jax.experimental import pallas as pl
from jax.experimental.pallas import tpu as pltpu

F32 = jnp.float32
BF16 = jnp.bfloat16
I32 = jnp.int32

D_MODEL = 1024
DEPTH = 1
PAGE_SIZE = 128
A_HEADS = 4
A_QK_DIM = 64
A_HEAD_DIM = 128
A_WIDTH = A_HEADS * A_HEAD_DIM
R_HEADS = 4
R_KEY_DIM = 128
R_VAL_DIM = 128
R_WIDTH = R_HEADS * R_VAL_DIM
IN_WIDTH = 3 * A_WIDTH + 2 * R_HEADS * R_KEY_DIM + 2 * R_WIDTH
RET_CHUNK = 128
ALIBI_SLOPES = tuple(2.0 ** (-8.0 * (h + 1) / A_HEADS) for h in range(A_HEADS))
RET_GAMMAS = tuple(1.0 - 2.0 ** (-5.0 - h) for h in range(R_HEADS))
N_EXPERTS = 32
TOP_K = 4
D_FF = D_MODEL
SWIGLU_LIMIT = 7.0
SWIGLU_ALPHA = 1.702
LN_EPS = 1e-5
MASK_VALUE = -1e30
DN_ALPHA = (2.0 * DEPTH) ** 0.25

LANES = 128
BF16_SUBLANES = 16
VMEM_LIMIT = 56 * 1024 * 1024

EXPERT_ROWS = 512
WIN = 64
SORT_CHUNK = 1024
COMBINE_CHUNK = 1024
MAX_TAIL_GROUPS = (WIN + EXPERT_ROWS) // BF16_SUBLANES


def _cparams(sem, vmem=VMEM_LIMIT, flags=None):
    return pltpu.CompilerParams(dimension_semantics=sem, vmem_limit_bytes=vmem, flags=flags)


def _ada_kernel(c_ref, w_ref, b_ref, o_ref):
    c = c_ref[...]
    s = (c * jax.nn.sigmoid(c)).astype(BF16)
    o_ref[...] = jnp.dot(s, w_ref[...].astype(BF16), preferred_element_type=F32) + b_ref[...]


def ada_modulation(c, w_ada, b_ada):
    rows = c.shape[0]
    n = w_ada.shape[1]
    tn = 1024
    return pl.pallas_call(
        _ada_kernel,
        out_shape=jax.ShapeDtypeStruct((rows, n), F32),
        grid=(n // tn,),
        in_specs=[pl.BlockSpec((rows, D_MODEL), lambda j: (0, 0)),
                  pl.BlockSpec((D_MODEL, tn), lambda j: (0, j)),
                  pl.BlockSpec((1, tn), lambda j: (0, j))],
        out_specs=pl.BlockSpec((rows, tn), lambda j: (0, j)),
        compiler_params=_cparams(("arbitrary",)),
        name="ada_modulation",
    )(c, w_ada, b_ada.reshape(1, n))


def _inproj_kernel(x_ref, mod_ref, w_ref, aq_ref, ak_ref, av_ref, akb_ref, avb_ref,
                   rq_ref, rk_ref, rv_ref, rg_ref):
    x = x_ref[0]
    sh1 = mod_ref[0, 0:1, :]
    sc1 = mod_ref[0, 1:2, :]
    u = (x * (1.0 + sc1) + sh1).astype(BF16)

    def proj(c):
        return jnp.dot(u, w_ref[:, c * A_WIDTH:(c + 1) * A_WIDTH], preferred_element_type=F32)

    ts = x.shape[0]
    aq_ref[0] = (proj(0) * (A_QK_DIM ** -0.5)).astype(BF16)
    k = proj(1)
    akb_ref[0] = k.astype(BF16)
    v = proj(2)
    avb_ref[0] = v.astype(BF16)
    for h in range(A_HEADS):
        ak_ref[0, pl.ds(h, ts, stride=A_HEADS), :] = k[:, h * A_HEAD_DIM:(h + 1) * A_HEAD_DIM]
        av_ref[0, pl.ds(h, ts, stride=A_HEADS), :] = v[:, h * A_HEAD_DIM:(h + 1) * A_HEAD_DIM]
    rq_ref[0] = proj(3).astype(BF16)
    rk_ref[0] = (proj(4) * (R_KEY_DIM ** -0.5)).astype(BF16)
    rv_ref[0] = proj(5).astype(BF16)
    rg_ref[0] = proj(6).astype(BF16)


def inproj_prompt(x, mod, w_in_bf16):
    b, s, _ = x.shape
    ts = min(512, s)
    blk = lambda: pl.BlockSpec((1, ts, A_WIDTH), lambda i, j: (i, j, 0))
    kvblk = lambda: pl.BlockSpec((1, ts * A_HEADS, A_HEAD_DIM), lambda i, j: (i, j, 0))
    f32o = jax.ShapeDtypeStruct((b, s * A_HEADS, A_HEAD_DIM), F32)
    b16o = jax.ShapeDtypeStruct((b, s, A_WIDTH), BF16)
    return pl.pallas_call(
        _inproj_kernel,
        out_shape=(b16o, f32o, f32o, b16o, b16o, b16o, b16o, b16o, b16o),
        grid=(b, s // ts),
        in_specs=[pl.BlockSpec((1, ts, D_MODEL), lambda i, j: (i, j, 0)),
                  pl.BlockSpec((1, 6, D_MODEL), lambda i, j: (i, 0, 0)),
                  pl.BlockSpec((D_MODEL, IN_WIDTH), lambda i, j: (0, 0))],
        out_specs=(blk(), kvblk(), kvblk()) + tuple(blk() for _ in range(6)),
        compiler_params=_cparams(("arbitrary", "arbitrary")),
        name="inproj_prompt",
    )(x, mod, w_in_bf16)


def _diff_lambda(lam_ref, lam_init):
    lp = lam_ref[...]
    e1 = jnp.exp(jnp.sum(lp[0:1, :] * lp[1:2, :], axis=-1, keepdims=True))
    e2 = jnp.exp(jnp.sum(lp[2:3, :] * lp[3:4, :], axis=-1, keepdims=True))
    return e1 - e2 + lam_init


def _attn_kernel(slopes_ref, q_ref, k_ref, v_ref, lam_ref, o_ref, bias_sc, v1_sc, *, tq, tk, lam_init):
    h = pl.program_id(1)
    i = pl.program_id(2)
    slope = slopes_ref[h]
    s_len = k_ref.shape[1]
    row = lax.broadcasted_iota(I32, (tq, tk), 0)
    col = lax.broadcasted_iota(I32, (tq, tk), 1)

    @pl.when(i == 0)
    def _():
        bias_sc[...] = -slope * (row - col).astype(F32)
        v1_sc[:, :A_HEAD_DIM] = v_ref[0]
        v1_sc[:, A_HEAD_DIM:] = jnp.ones((s_len, A_HEAD_DIM), BF16)

    q = q_ref[0]
    lane = lax.broadcasted_iota(I32, (tq, A_HEAD_DIM), 1)
    zero = jnp.zeros_like(q)
    q1 = jnp.where(lane < A_QK_DIM, q, zero)
    q2 = jnp.where(lane >= A_QK_DIM, q, zero)
    nt = (((1,), (1,)), ((), ()))

    def step(qm, kb, vb, bias, cj, m, a):
        s = lax.dot_general(qm, kb, nt, preferred_element_type=F32) + bias
        m_new = jnp.maximum(m, jnp.max(s, axis=-1, keepdims=True) + cj)
        p = jnp.exp(s - (m_new - cj))
        a = a * jnp.exp(m - m_new) + jnp.dot(p.astype(BF16), vb, preferred_element_type=F32)
        return m_new, a

    mi = jnp.full((tq, 1), MASK_VALUE, F32)
    ai = jnp.zeros((tq, 2 * A_HEAD_DIM), F32)
    lam = _diff_lambda(lam_ref, lam_init)

    def run(qi):
        m1, a1, m2, a2 = mi, ai, mi, ai
        for j in range(qi * (tq // tk)):
            kb = k_ref[0, j * tk:(j + 1) * tk, :]
            vb = v1_sc[j * tk:(j + 1) * tk, :]
            cj = slope * float(j * tk - qi * tq)
            bias = bias_sc[...]
            m1, a1 = step(q1, kb, vb, bias, cj, m1, a1)
            m2, a2 = step(q2, kb, vb, bias, cj, m2, a2)
        for d in range(tq // tk):
            lo = qi * tq + d * tk
            kb = k_ref[0, lo:lo + tk, :]
            vb = v1_sc[lo:lo + tk, :]
            bias = jnp.where(col + d * tk <= row, bias_sc[...], MASK_VALUE)
            cj = slope * float(d * tk)
            m1, a1 = step(q1, kb, vb, bias, cj, m1, a1)
            m2, a2 = step(q2, kb, vb, bias, cj, m2, a2)
        o1 = a1[:, :A_HEAD_DIM] / a1[:, A_HEAD_DIM:A_HEAD_DIM + 1]
        o2 = a2[:, :A_HEAD_DIM] / a2[:, A_HEAD_DIM:A_HEAD_DIM + 1]
        o_ref[0] = (o1 - lam * o2).astype(o_ref.dtype)

    for qi in range(s_len // tq):
        pl.when(i == qi)(functools.partial(run, qi))


def diff_attn_prompt(aq, akb, avb, lam_params, lam_init):
    b, s, _ = aq.shape
    tq = min(512, s)
    tk = tq
    slopes = jnp.asarray(ALIBI_SLOPES, F32)
    grid_spec = pltpu.PrefetchScalarGridSpec(
        num_scalar_prefetch=1,
        grid=(b, A_HEADS, s // tq),
        in_specs=[pl.BlockSpec((1, tq, A_HEAD_DIM), lambda bi, h, i, sl: (bi, i, h)),
                  pl.BlockSpec((1, s, A_HEAD_DIM), lambda bi, h, i, sl: (bi, 0, h)),
                  pl.BlockSpec((1, s, A_HEAD_DIM), lambda bi, h, i, sl: (bi, 0, h)),
                  pl.BlockSpec((4, A_QK_DIM), lambda bi, h, i, sl: (0, 0))],
        out_specs=pl.BlockSpec((1, tq, A_HEAD_DIM), lambda bi, h, i, sl: (bi, i, h)),
        scratch_shapes=[pltpu.VMEM((tq, tk), F32), pltpu.VMEM((s, 2 * A_HEAD_DIM), BF16)],
    )
    return pl.pallas_call(
        functools.partial(_attn_kernel, tq=tq, tk=tk, lam_init=lam_init),
        out_shape=jax.ShapeDtypeStruct((b, s, A_WIDTH), BF16),
        grid_spec=grid_spec,
        compiler_params=_cparams(("arbitrary", "arbitrary", "arbitrary")),
        name="diff_attn_prompt",
    )(slopes, aq, akb, avb, lam_params)


def _ret_kernel(lg_ref, q_ref, k_ref, v_ref, o_ref, st_ref, *, n_chunks):
    h = pl.program_id(1)
    lg = lg_ref[h]
    c = RET_CHUNK
    row = lax.broadcasted_iota(I32, (c, c), 0)
    col = lax.broadcasted_iota(I32, (c, c), 1)
    rel = (row - col).astype(F32)
    decay = jnp.where(rel >= 0, jnp.exp(lg * jnp.maximum(rel, 0.0)), 0.0)
    rowf = lax.broadcasted_iota(I32, (c, R_VAL_DIM), 0).astype(F32)
    q_dec = jnp.exp(lg * (rowf + 1.0))
    k_dec = jnp.exp(lg * (c - 1.0 - rowf))
    lgc = jnp.full((1, 1), c, F32) * lg
    s_dec = jnp.exp(lgc)
    nt = (((1,), (1,)), ((), ()))
    tn = (((0,), (0,)), ((), ()))
    state = jnp.zeros((R_KEY_DIM, R_VAL_DIM), F32)
    for n in range(n_chunks):
        q = q_ref[0, n * c:(n + 1) * c, :]
        k = k_ref[0, n * c:(n + 1) * c, :]
        v = v_ref[0, n * c:(n + 1) * c, :]
        inner = lax.dot_general(q, k, nt, preferred_element_type=F32) * decay
        o = (jnp.dot(inner.astype(BF16), v, preferred_element_type=F32)
             + jnp.dot(q, state.astype(BF16), preferred_element_type=F32) * q_dec)
        o_ref[0, n * c:(n + 1) * c, :] = o.astype(o_ref.dtype)
        kd = (k.astype(F32) * k_dec).astype(BF16)
        state = state * s_dec + lax.dot_general(kd, v, tn, preferred_element_type=F32)
    st_ref[0, 0] = state


def retention_prompt(rq, rk, rv):
    b, s, _ = rq.shape
    lg = jnp.asarray([math.log(g) for g in RET_GAMMAS], F32)
    blk = lambda: pl.BlockSpec((1, s, R_VAL_DIM), lambda bi, h, lgr: (bi, 0, h))
    grid_spec = pltpu.PrefetchScalarGridSpec(
        num_scalar_prefetch=1,
        grid=(b, R_HEADS),
        in_specs=[blk(), blk(), blk()],
        out_specs=(blk(), pl.BlockSpec((1, 1, R_KEY_DIM, R_VAL_DIM), lambda bi, h, lgr: (bi, h, 0, 0))),
    )
    return pl.pallas_call(
        functools.partial(_ret_kernel, n_chunks=s // RET_CHUNK),
        out_shape=(jax.ShapeDtypeStruct((b, s, R_WIDTH), BF16),
                   jax.ShapeDtypeStruct((b, R_HEADS, R_KEY_DIM, R_VAL_DIM), F32)),
        grid_spec=grid_spec,
        compiler_params=_cparams(("arbitrary", "arbitrary")),
        name="retention_prompt",
    )(lg, rq, rk, rv)


def _merge_kernel(ao_ref, ro_ref, rg_ref, x_ref, mod_ref, ag_ref, rgn_ref, rbn_ref, wout_ref,
                  g1_ref, b1_ref, wrh_ref, wrl_ref, br_ref,
                  x1_ref, h2_ref, idx_ref, w_ref, rank_ref, cnt_ref, *, ts, lam_init, per_row_mod):
    if per_row_mod:
        ao, ro, rg, x = ao_ref[...], ro_ref[...], rg_ref[...], x_ref[...]
        gate1, shift2, scale2 = mod_ref[:, 2, :], mod_ref[:, 3, :], mod_ref[:, 4, :]
    else:
        ao, ro, rg, x = ao_ref[0], ro_ref[0], rg_ref[0], x_ref[0]
        gate1, shift2, scale2 = mod_ref[0, 2:3, :], mod_ref[0, 3:4, :], mod_ref[0, 4:5, :]
    ao = ao.astype(F32)
    ro = ro.astype(F32)
    rg = rg.astype(F32)
    pieces = []
    for h in range(A_HEADS):
        a = ao[:, h * A_HEAD_DIM:(h + 1) * A_HEAD_DIM]
        ms = jnp.mean(a * a, axis=-1, keepdims=True)
        pieces.append((a * lax.rsqrt(ms + LN_EPS) * ag_ref[...] * (1.0 - lam_init)).astype(BF16))
    for h in range(R_HEADS):
        r = ro[:, h * R_VAL_DIM:(h + 1) * R_VAL_DIM]
        mu = jnp.mean(r, axis=-1, keepdims=True)
        d = r - mu
        var = jnp.mean(d * d, axis=-1, keepdims=True)
        rn = d * lax.rsqrt(var + LN_EPS) * rgn_ref[h:h + 1, :] + rbn_ref[h:h + 1, :]
        g = rg[:, h * R_VAL_DIM:(h + 1) * R_VAL_DIM]
        pieces.append(((g * jax.nn.sigmoid(g)) * rn).astype(BF16))
    cat = jnp.concatenate(pieces, axis=-1)
    mix = jnp.dot(cat, wout_ref[...], preferred_element_type=F32)
    y = DN_ALPHA * x + (1.0 + gate1) * mix
    mu = jnp.mean(y, axis=-1, keepdims=True)
    d = y - mu
    var = jnp.mean(d * d, axis=-1, keepdims=True)
    x1 = d * lax.rsqrt(var + LN_EPS) * g1_ref[...] + b1_ref[...]
    h2 = x1 * (1.0 + scale2) + shift2
    if per_row_mod:
        x1_ref[...] = x1
    else:
        x1_ref[0] = x1
    h_hi = h2.astype(BF16)
    h2_ref[...] = h_hi
    h_lo = (h2 - h_hi.astype(F32)).astype(BF16)
    logits = (jnp.dot(h_hi, wrh_ref[...], preferred_element_type=F32)
              + jnp.dot(h_lo, wrh_ref[...], preferred_element_type=F32)
              + jnp.dot(h_hi, wrl_ref[...], preferred_element_type=F32)) + br_ref[...]
    lane = lax.broadcasted_iota(I32, (ts, N_EXPERTS), 1).astype(F32)
    lane4 = lax.broadcasted_iota(I32, (ts, TOP_K), 1)
    vals = logits
    member = jnp.zeros((ts, N_EXPERTS), F32)
    sels, tops = [], []
    idx_out = jnp.zeros((ts, TOP_K), I32)
    for k in range(TOP_K):
        m = jnp.max(vals, axis=-1, keepdims=True)
        first = jnp.min(jnp.where(vals == m, lane, float(N_EXPERTS)), axis=-1, keepdims=True)
        sel = lane == first
        sels.append(sel)
        tops.append(m)
        member = member + sel.astype(F32)
        vals = jnp.where(sel, -jnp.inf, vals)
        idx_out = jnp.where(lane4 == k, first.astype(I32), idx_out)
    es = [jnp.exp(t - tops[0]) for t in tops]
    den = es[0] + es[1] + es[2] + es[3]
    w_out = jnp.zeros((ts, TOP_K), F32)
    for k in range(TOP_K):
        w_out = jnp.where(lane4 == k, es[k] / den, w_out)
    tril = (lax.broadcasted_iota(I32, (ts, ts), 1) < lax.broadcasted_iota(I32, (ts, ts), 0))
    prefix = jnp.dot(tril.astype(BF16), member.astype(BF16), preferred_element_type=F32)
    rank_out = jnp.zeros((ts, TOP_K), I32)
    for k in range(TOP_K):
        rk = jnp.sum(jnp.where(sels[k], prefix, 0.0), axis=-1, keepdims=True)
        rank_out = jnp.where(lane4 == k, rk.astype(I32), rank_out)
    idx_ref[...] = idx_out
    w_ref[...] = w_out
    rank_ref[...] = rank_out
    cnt_ref[0] = jnp.sum(member, axis=0, keepdims=True).astype(I32)


def merge_router(ao, ro, rg, x, mod, p, lam_init, ts, per_row_mod):
    if per_row_mod:
        t = x.shape[0]
        grid = (t // ts,)
        tok = lambda w: pl.BlockSpec((ts, w), lambda j: (j, 0))
        in_tok = [tok(A_WIDTH), tok(R_WIDTH), tok(R_WIDTH), tok(D_MODEL),
                  pl.BlockSpec((ts, 6, D_MODEL), lambda j: (j, 0, 0))]
        x1_spec = tok(D_MODEL)
        x1_shape = jax.ShapeDtypeStruct((t, D_MODEL), F32)
        flat = lambda j: j
        const = lambda shape: pl.BlockSpec(shape, lambda j: (0,) * len(shape))
        sem = ("arbitrary",)
    else:
        b, s, _ = x.shape
        t = b * s
        nts = s // ts
        grid = (b, nts)
        tok = lambda w: pl.BlockSpec((1, ts, w), lambda i, j: (i, j, 0))
        in_tok = [tok(A_WIDTH), tok(R_WIDTH), tok(R_WIDTH), tok(D_MODEL),
                  pl.BlockSpec((1, 6, D_MODEL), lambda i, j: (i, 0, 0))]
        x1_spec = tok(D_MODEL)
        x1_shape = jax.ShapeDtypeStruct((b, s, D_MODEL), F32)
        flat = lambda i, j: i * nts + j
        const = lambda shape: pl.BlockSpec(shape, lambda i, j: (0,) * len(shape))
        sem = ("arbitrary", "arbitrary")
    nt = t // ts
    small = lambda: pl.BlockSpec((ts, TOP_K), lambda *a: (flat(*a), 0))
    return pl.pallas_call(
        functools.partial(_merge_kernel, ts=ts, lam_init=lam_init, per_row_mod=per_row_mod),
        out_shape=(x1_shape,
                   jax.ShapeDtypeStruct((t, D_MODEL), BF16),
                   jax.ShapeDtypeStruct((t, TOP_K), I32),
                   jax.ShapeDtypeStruct((t, TOP_K), F32),
                   jax.ShapeDtypeStruct((t, TOP_K), I32),
                   jax.ShapeDtypeStruct((nt, 1, N_EXPERTS), I32)),
        grid=grid,
        in_specs=in_tok + [const((1, A_HEAD_DIM)), const((R_HEADS, R_VAL_DIM)), const((R_HEADS, R_VAL_DIM)),
                           const((D_MODEL, D_MODEL)), const((1, D_MODEL)), const((1, D_MODEL)),
                           const((D_MODEL, N_EXPERTS)), const((D_MODEL, N_EXPERTS)), const((1, N_EXPERTS))],
        out_specs=(x1_spec,
                   pl.BlockSpec((ts, D_MODEL), lambda *a: (flat(*a), 0)),
                   small(), small(), small(),
                   pl.BlockSpec((1, 1, N_EXPERTS), lambda *a: (flat(*a), 0, 0))),
        compiler_params=_cparams(sem),
        name="merge_router_rows" if per_row_mod else "merge_router",
    )(ao, ro, rg, x, mod, p["a_norm_g"], p["r_norm_g"], p["r_norm_b"], p["w_out"],
      p["ln1_g"], p["ln1_b"], p["w_router_hi"], p["w_router_lo"], p["b_router"])


def _round_up(x, m):
    return (x + m - 1) // m * m


def _max_windows(ts):
    return N_EXPERTS + (TOP_K * ts + N_EXPERTS * (BF16_SUBLANES - 1) + WIN - 1) // WIN


def _local_rows(ts):
    return _round_up(TOP_K * ts + N_EXPERTS * (WIN - 1), COMBINE_CHUNK)


def _num_blocks(t, ts):
    nt = t // ts
    rows = TOP_K * t + nt * N_EXPERTS * (BF16_SUBLANES - 1) + N_EXPERTS * (WIN + EXPERT_ROWS - 1)
    return rows // EXPERT_ROWS + 1


def routing_plan(cnt, ts, n_blocks):
    nt = cnt.shape[0]
    maxw = _max_windows(ts)
    c16 = _round_up(cnt, BF16_SUBLANES)
    nwin = (c16 + WIN - 1) // WIN
    l64 = nwin * WIN
    loff = jnp.cumsum(l64, axis=1) - l64
    n_chunk = (jnp.sum(l64, axis=1) + SORT_CHUNK - 1) // SORT_CHUNK
    tot = jnp.sum(c16, axis=0)
    region = _round_up(tot + WIN, EXPERT_ROWS)
    rend = jnp.cumsum(region)
    goff = (rend - region)[None, :] + jnp.cumsum(c16, axis=0) - c16
    n_used = (rend[-1] // EXPERT_ROWS).astype(I32)
    blk = jnp.arange(n_blocks, dtype=I32)
    block_e = jnp.minimum(jnp.sum((rend // EXPERT_ROWS)[None, :] <= blk[:, None], axis=1), N_EXPERTS - 1).astype(I32)
    cw = jnp.cumsum(nwin, axis=1)
    wi = jnp.arange(maxw, dtype=I32)
    we = jnp.minimum(jnp.sum(cw[:, None, :] <= wi[None, :, None], axis=2), N_EXPERTS - 1)
    wk = wi[None, :] - jnp.take_along_axis(cw - nwin, we, axis=1)
    win_src = jnp.take_along_axis(loff, we, axis=1) + wk * WIN
    win_dst = jnp.take_along_axis(goff, we, axis=1) + wk * WIN
    valid = wi[None, :] < cw[:, -1:]
    win_src = jnp.where(valid, win_src, 0).astype(I32).reshape(-1)
    win_dst = jnp.where(valid, win_dst, 0).astype(I32).reshape(-1)
    tail_n = (region - tot) // BF16_SUBLANES
    zc = jnp.cumsum(tail_n)
    zi = jnp.arange(N_EXPERTS * MAX_TAIL_GROUPS, dtype=I32)
    ze = jnp.minimum(jnp.sum(zc[None, :] <= zi[:, None], axis=1), N_EXPERTS - 1)
    zdst = (rend - region + tot)[ze] + (zi - (zc - tail_n)[ze]) * BF16_SUBLANES
    zdst = jnp.where(zi < zc[-1], zdst, 0).astype(I32)
    return dict(n_win=cw[:, -1].astype(I32), win_src=win_src, win_dst=win_dst, n_chunk=n_chunk.astype(I32),
                loff=loff.astype(I32), block_e=block_e, n_used=n_used.reshape(1),
                n_zero=zc[-1].astype(I32).reshape(1), zero_dst=zdst)


def _wait_copies(n, src, dst, sem, rows=WIN):
    def body(i, c):
        pltpu.make_async_copy(src.at[pl.ds(0, rows)], dst.at[pl.ds(0, rows)], sem).wait()
        return c
    lax.fori_loop(0, n, body, 0)


def _dispatch_kernel(nwin_ref, src_ref, dst_ref, nchunk_ref, nzero_ref, zdst_ref, nused_ref,
                     h2_ref, idxt_ref, rankt_ref, loff_ref, xbuf_ref, sbuf, zbuf, sem, zsem,
                     *, ts, maxw, n_blocks):
    j = pl.program_id(0)
    slot = j % 2
    cur = sbuf.at[slot]

    @pl.when(j == 0)
    def _():
        sbuf[...] = jnp.zeros_like(sbuf)
        zbuf[...] = jnp.zeros_like(zbuf)
        nz = nzero_ref[0]
        nu = nused_ref[0]

        def tail(i, c):
            d = pl.multiple_of(zdst_ref[i], BF16_SUBLANES)
            pltpu.make_async_copy(zbuf.at[pl.ds(0, BF16_SUBLANES)], xbuf_ref.at[pl.ds(d, BF16_SUBLANES)],
                                  zsem.at[0]).start()
            return c

        def block(b, c):
            d = pl.multiple_of(b * EXPERT_ROWS, EXPERT_ROWS)
            pltpu.make_async_copy(zbuf, xbuf_ref.at[pl.ds(d, EXPERT_ROWS)], zsem.at[1]).start()
            return c

        lax.fori_loop(0, nz, tail, 0)
        lax.fori_loop(nu, n_blocks, block, 0)
        _wait_copies(nz, zbuf, xbuf_ref, zsem.at[0], BF16_SUBLANES)
        _wait_copies(n_blocks - nu, zbuf, xbuf_ref, zsem.at[1], EXPERT_ROWS)

    e_iota = lax.broadcasted_iota(I32, (N_EXPERTS, ts), 0)
    loff = loff_ref[0].astype(F32)
    ld = []
    for k in range(TOP_K):
        hit = idxt_ref[k:k + 1, :] == e_iota
        off = jnp.sum(jnp.where(hit, loff, 0.0), axis=0, keepdims=True)
        ld.append(off.astype(I32) + rankt_ref[k:k + 1, :])
    x = h2_ref[...]
    row = lax.broadcasted_iota(I32, (SORT_CHUNK, ts), 0)

    def chunk(c, carry):
        r0 = pl.multiple_of(c * SORT_CHUNK, SORT_CHUNK)
        rows = row + r0
        onehot = (ld[0] == rows) | (ld[1] == rows) | (ld[2] == rows) | (ld[3] == rows)
        p = jnp.where(onehot, 1.0, 0.0).astype(BF16)
        cur[pl.ds(r0, SORT_CHUNK), :] = jnp.dot(p, x, preferred_element_type=F32).astype(BF16)
        return carry

    lax.fori_loop(0, nchunk_ref[j], chunk, 0)

    @pl.when(j > 0)
    def _():
        _wait_copies(nwin_ref[j - 1], sbuf.at[1 - slot], xbuf_ref, sem.at[1 - slot])

    n = nwin_ref[j]

    def issue(i, carry):
        s = pl.multiple_of(src_ref[j * maxw + i], BF16_SUBLANES)
        d = pl.multiple_of(dst_ref[j * maxw + i], BF16_SUBLANES)
        pltpu.make_async_copy(cur.at[pl.ds(s, WIN)], xbuf_ref.at[pl.ds(d, WIN)], sem.at[slot]).start()
        return carry

    lax.fori_loop(0, n, issue, 0)

    @pl.when(j == pl.num_programs(0) - 1)
    def _():
        _wait_copies(n, cur, xbuf_ref, sem.at[slot])


def moe_dispatch(h2, idx_t, rank_t, plan, ts, n_blocks):
    t = h2.shape[0]
    nt = t // ts
    maxw = _max_windows(ts)
    lrows = _local_rows(ts)
    grid_spec = pltpu.PrefetchScalarGridSpec(
        num_scalar_prefetch=7,
        grid=(nt,),
        in_specs=[pl.BlockSpec((ts, D_MODEL), lambda j, *_: (j, 0)),
                  pl.BlockSpec((TOP_K, ts), lambda j, *_: (0, j)),
                  pl.BlockSpec((TOP_K, ts), lambda j, *_: (0, j)),
                  pl.BlockSpec((1, N_EXPERTS, 1), lambda j, *_: (j, 0, 0))],
        out_specs=pl.BlockSpec(memory_space=pl.ANY),
        scratch_shapes=[pltpu.VMEM((2, lrows, D_MODEL), BF16), pltpu.VMEM((EXPERT_ROWS, D_MODEL), BF16),
                        pltpu.SemaphoreType.DMA((2,)), pltpu.SemaphoreType.DMA((2,))],
    )
    return pl.pallas_call(
        functools.partial(_dispatch_kernel, ts=ts, maxw=maxw, n_blocks=n_blocks),
        out_shape=jax.ShapeDtypeStruct((n_blocks * EXPERT_ROWS, D_MODEL), BF16),
        grid_spec=grid_spec,
        compiler_params=_cparams(("arbitrary",)),
        name="moe_dispatch",
    )(plan["n_win"], plan["win_src"], plan["win_dst"], plan["n_chunk"],
      plan["n_zero"], plan["zero_dst"], plan["n_used"],
      h2, idx_t, rank_t, plan["loff"].reshape(nt, N_EXPERTS, 1))


def _expert_kernel(be_ref, nu_ref, x_ref, wg_ref, bg_ref, wd_ref, bd_ref, y_ref):
    b = pl.program_id(0)

    @pl.when(b < nu_ref[0])
    def _():
        gu = jnp.dot(x_ref[...], wg_ref[0], preferred_element_type=F32) + bg_ref[0]
        glu = jnp.minimum(gu[:, :D_FF], SWIGLU_LIMIT)
        lin = jnp.clip(gu[:, D_FF:], -SWIGLU_LIMIT, SWIGLU_LIMIT)
        act = glu * jax.nn.sigmoid(SWIGLU_ALPHA * glu) * (lin + 1.0)
        y = jnp.dot(act.astype(BF16), wd_ref[0], preferred_element_type=F32) + bd_ref[0]
        y_ref[...] = y.astype(BF16)

    @pl.when(b >= nu_ref[0])
    def _():
        y_ref[...] = jnp.zeros_like(y_ref)


def moe_experts(xbuf, plan, w_gu, b_gu, w_dn, b_dn, n_blocks):
    row_map = lambda b, be, nu: (jnp.minimum(b, nu[0] - 1), 0)
    out_map = lambda b, be, nu: (b, 0)
    e_map3 = lambda b, be, nu: (be[b], 0, 0)
    grid_spec = pltpu.PrefetchScalarGridSpec(
        num_scalar_prefetch=2,
        grid=(n_blocks,),
        in_specs=[pl.BlockSpec((EXPERT_ROWS, D_MODEL), row_map),
                  pl.BlockSpec((1, D_MODEL, 2 * D_FF), e_map3),
                  pl.BlockSpec((1, 1, 2 * D_FF), e_map3),
                  pl.BlockSpec((1, D_FF, D_MODEL), e_map3),
                  pl.BlockSpec((1, 1, D_MODEL), e_map3)],
        out_specs=pl.BlockSpec((EXPERT_ROWS, D_MODEL), out_map),
    )
    return pl.pallas_call(
        _expert_kernel,
        out_shape=jax.ShapeDtypeStruct((n_blocks * EXPERT_ROWS, D_MODEL), BF16),
        grid_spec=grid_spec,
        compiler_params=_cparams(("arbitrary",)),
        name="moe_experts",
    )(plan["block_e"], plan["n_used"], xbuf, w_gu, b_gu, w_dn, b_dn)


def _combine_kernel(nwin_ref, src_ref, dst_ref, nchunk_ref,
                    x1_ref, mod_ref, idx_ref, w_ref, rank_ref, loff_ref, g2_ref, b2_ref, ybuf_ref,
                    o_ref, ywin, sem, *, ts, maxw, per_row_mod):
    j = pl.program_id(0)
    slot = j % 2
    cur = ywin.at[slot]

    def fetch(t, sl):
        def issue(i, carry):
            s = pl.multiple_of(src_ref[t * maxw + i], BF16_SUBLANES)
            d = pl.multiple_of(dst_ref[t * maxw + i], BF16_SUBLANES)
            pltpu.make_async_copy(ybuf_ref.at[pl.ds(d, WIN)], ywin.at[sl].at[pl.ds(s, WIN)], sem.at[sl]).start()
            return carry
        lax.fori_loop(0, nwin_ref[t], issue, 0)

    @pl.when(j == 0)
    def _():
        ywin[...] = jnp.zeros_like(ywin)
        fetch(0, 0)

    @pl.when(j + 1 < pl.num_programs(0))
    def _():
        fetch(j + 1, 1 - slot)

    e_iota = lax.broadcasted_iota(I32, (ts, N_EXPERTS), 1)
    loff = loff_ref[0].astype(F32)
    ld, wk = [], []
    for k in range(TOP_K):
        hit = idx_ref[:, k:k + 1] == e_iota
        off = jnp.sum(jnp.where(hit, loff, 0.0), axis=-1, keepdims=True)
        ldk = off.astype(I32) + rank_ref[:, k:k + 1]
        ld.append(jnp.broadcast_to(ldk, (ts, LANES)))
        wk.append(jnp.broadcast_to(w_ref[:, k:k + 1], (ts, LANES)))
    lane = lax.broadcasted_iota(I32, (ts, LANES), 1)
    _wait_copies(nwin_ref[j], ybuf_ref, cur, sem.at[slot])

    def chunk(c, f):
        r0 = pl.multiple_of(c * COMBINE_CHUNK, COMBINE_CHUNK)
        groups = []
        for g in range(COMBINE_CHUNK // LANES):
            cols = lane + (r0 + g * LANES)
            pw = jnp.zeros((ts, LANES), F32)
            for k in range(TOP_K):
                pw = pw + jnp.where(ld[k] == cols, wk[k], 0.0)
            groups.append(pw.astype(BF16))
        pw = jnp.concatenate(groups, axis=1)
        return f + jnp.dot(pw, cur[pl.ds(r0, COMBINE_CHUNK), :], preferred_element_type=F32)

    n_chunk = (nchunk_ref[j] * SORT_CHUNK + COMBINE_CHUNK - 1) // COMBINE_CHUNK
    f = lax.fori_loop(0, n_chunk, chunk, jnp.zeros((ts, D_MODEL), F32))
    if per_row_mod:
        x1 = x1_ref[...]
        gate2 = mod_ref[:, 5, :]
    else:
        x1 = x1_ref[...]
        gate2 = mod_ref[0, 5:6, :]
    y = DN_ALPHA * x1 + (1.0 + gate2) * f
    mu = jnp.mean(y, axis=-1, keepdims=True)
    d = y - mu
    var = jnp.mean(d * d, axis=-1, keepdims=True)
    o_ref[...] = d * lax.rsqrt(var + LN_EPS) * g2_ref[...] + b2_ref[...]


def moe_combine(x1, mod, idx, w, rank, plan, ybuf, ln2_g, ln2_b, ts, tiles_per_mod, per_row_mod):
    t = x1.shape[0]
    nt = t // ts
    maxw = _max_windows(ts)
    lrows = _local_rows(ts)
    if per_row_mod:
        mod_spec = pl.BlockSpec((ts, 6, D_MODEL), lambda j, *_: (j, 0, 0))
    else:
        mod_spec = pl.BlockSpec((1, 6, D_MODEL), lambda j, *_: (j // tiles_per_mod, 0, 0))
    small = lambda: pl.BlockSpec((ts, TOP_K), lambda j, *_: (j, 0))
    grid_spec = pltpu.PrefetchScalarGridSpec(
        num_scalar_prefetch=4,
        grid=(nt,),
        in_specs=[pl.BlockSpec((ts, D_MODEL), lambda j, *_: (j, 0)),
                  mod_spec, small(), small(), small(),
                  pl.BlockSpec((1, 1, N_EXPERTS), lambda j, *_: (j, 0, 0)),
                  pl.BlockSpec((1, D_MODEL), lambda j, *_: (0, 0)),
                  pl.BlockSpec((1, D_MODEL), lambda j, *_: (0, 0)),
                  pl.BlockSpec(memory_space=pl.ANY)],
        out_specs=pl.BlockSpec((ts, D_MODEL), lambda j, *_: (j, 0)),
        scratch_shapes=[pltpu.VMEM((2, lrows, D_MODEL), BF16), pltpu.SemaphoreType.DMA((2,))],
    )
    return pl.pallas_call(
        functools.partial(_combine_kernel, ts=ts, maxw=maxw, per_row_mod=per_row_mod),
        out_shape=jax.ShapeDtypeStruct((t, D_MODEL), F32),
        grid_spec=grid_spec,
        compiler_params=_cparams(("arbitrary",)),
        name="moe_combine_rows" if per_row_mod else "moe_combine",
    )(plan["n_win"], plan["win_src"], plan["win_dst"], plan["n_chunk"],
      x1, mod, idx, w, rank, plan["loff"].reshape(nt, 1, N_EXPERTS), ln2_g, ln2_b, ybuf)


def moe_block(x1_flat, h2, idx, w, rank, cnt, mod, p, ts, tiles_per_mod, per_row_mod):
    t = h2.shape[0]
    n_blocks = _num_blocks(t, ts)
    plan = routing_plan(cnt.reshape(-1, N_EXPERTS), ts, n_blocks)
    xbuf = moe_dispatch(h2, idx.T, rank.T, plan, ts, n_blocks)
    ybuf = moe_experts(xbuf, plan, p["w_gu"], p["b_gu"], p["w_dn"], p["b_dn"], n_blocks)
    return moe_combine(x1_flat, mod, idx, w, rank, plan, ybuf, p["ln2_g"], p["ln2_b"], ts,
                       tiles_per_mod, per_row_mod)


def _inproj_rows_kernel(x_ref, mod_ref, w_ref, h_ref):
    u = (x_ref[...] * (1.0 + mod_ref[:, 1, :]) + mod_ref[:, 0, :]).astype(BF16)
    h_ref[...] = jnp.dot(u, w_ref[...], preferred_element_type=F32)


def inproj_rows(x, mod, w_in_bf16):
    t = x.shape[0]
    tn = A_WIDTH
    return pl.pallas_call(
        _inproj_rows_kernel,
        out_shape=jax.ShapeDtypeStruct((t, IN_WIDTH), F32),
        grid=(IN_WIDTH // tn,),
        in_specs=[pl.BlockSpec((t, D_MODEL), lambda j: (0, 0)),
                  pl.BlockSpec((t, 6, D_MODEL), lambda j: (0, 0, 0)),
                  pl.BlockSpec((D_MODEL, tn), lambda j: (0, j))],
        out_specs=pl.BlockSpec((t, tn), lambda j: (0, j)),
        compiler_params=_cparams(("arbitrary",)),
        name="inproj_rows",
    )(x, mod, w_in_bf16)


def _decode_attn_kernel(pt_ref, q_ref, kn_ref, vn_ref, lam_ref, *rest, pages, n_pages, lam_init):
    del pt_ref
    k_refs = rest[:pages]
    v_refs = rest[pages:2 * pages]
    o_ref = rest[2 * pages]
    m_sc, l_sc, a_sc = rest[2 * pages + 1:]
    s_id = pl.program_id(1)
    n_steps = pl.num_programs(1)
    rows = PAGE_SIZE * A_HEADS
    r8 = 2 * A_HEADS

    @pl.when(s_id == 0)
    def _():
        m_sc[...] = jnp.full_like(m_sc, MASK_VALUE)
        l_sc[...] = jnp.zeros_like(l_sc)
        a_sc[...] = jnp.zeros_like(a_sc)

    q4 = q_ref[0] * (A_QK_DIM ** -0.5)
    q8 = jnp.concatenate([q4, q4], axis=0)
    rlane = lax.broadcasted_iota(I32, (r8, A_HEAD_DIM), 1)
    rrow = lax.broadcasted_iota(I32, (r8, A_HEAD_DIM), 0)
    keep = ((rrow < A_HEADS) & (rlane < A_QK_DIM)) | ((rrow >= A_HEADS) & (rlane >= A_QK_DIM))
    q8 = jnp.where(keep, q8, 0.0)
    q8b = q8.astype(BF16)

    srow = lax.broadcasted_iota(I32, (r8, rows), 0)
    scol = lax.broadcasted_iota(I32, (r8, rows), 1)
    head_r = srow & (A_HEADS - 1)
    own = (scol & (A_HEADS - 1)) == head_r
    slope = jnp.full((r8, rows), ALIBI_SLOPES[0], F32)
    for hh in range(1, A_HEADS):
        slope = jnp.where(head_r == hh, ALIBI_SLOPES[hh], slope)
    pos = scol >> 2
    q_pos = n_pages * PAGE_SIZE
    nt = (((1,), (1,)), ((), ()))

    scores = []
    for i in range(pages):
        page = s_id * pages + i
        dist = (q_pos - page * PAGE_SIZE - pos).astype(F32)
        kb = k_refs[i][0].astype(BF16)
        s = lax.dot_general(q8b, kb, nt, preferred_element_type=F32)
        scores.append(jnp.where(own, s - slope * dist, MASK_VALUE))
    m_old = m_sc[...]
    m_new = m_old
    for s in scores:
        m_new = jnp.maximum(m_new, jnp.max(s, axis=-1, keepdims=True))
    corr = jnp.exp(m_old - m_new)
    l = l_sc[...] * corr
    acc = a_sc[...] * corr
    for i, s in enumerate(scores):
        p = jnp.exp(s - m_new)
        l = l + jnp.sum(p, axis=-1, keepdims=True)
        acc = acc + jnp.dot(p.astype(BF16), v_refs[i][0].astype(BF16), preferred_element_type=F32)
    m_sc[...] = m_new
    l_sc[...] = l
    a_sc[...] = acc

    @pl.when(s_id == n_steps - 1)
    def _():
        kn = kn_ref[0]
        vn = vn_ref[0]
        k8 = jnp.concatenate([kn, kn], axis=0)
        v8 = jnp.concatenate([vn, vn], axis=0)
        s_new = jnp.sum(q8b.astype(F32) * k8.astype(BF16).astype(F32), axis=-1, keepdims=True)
        m_fin = jnp.maximum(m_new, s_new)
        c2 = jnp.exp(m_new - m_fin)
        p_new = jnp.exp(s_new - m_fin)
        l_fin = l * c2 + p_new
        a_fin = acc * c2 + p_new.astype(BF16).astype(F32) * v8.astype(BF16).astype(F32)
        o = a_fin / l_fin
        lam = _diff_lambda(lam_ref, lam_init)
        o_ref[0] = o[:A_HEADS] - lam * o[A_HEADS:]


def diff_attn_sample(q, k_new, v_new, cache_k, cache_v, page_table, lam_params, lam_init):
    bd, n_pages = page_table.shape
    pages = min(16, n_pages)
    rows = PAGE_SIZE * A_HEADS
    pt = page_table.reshape(-1)

    def page_spec(i):
        return pl.BlockSpec((1, rows, A_HEAD_DIM), lambda b, s, ptr: (ptr[b * n_pages + s * pages + i], 0, 0))

    tok = lambda: pl.BlockSpec((1, A_HEADS, A_HEAD_DIM), lambda b, s, ptr: (b, 0, 0))
    grid_spec = pltpu.PrefetchScalarGridSpec(
        num_scalar_prefetch=1,
        grid=(bd, n_pages // pages),
        in_specs=[tok(), tok(), tok(), pl.BlockSpec((4, A_QK_DIM), lambda b, s, ptr: (0, 0))]
        + [page_spec(i) for i in range(pages)] + [page_spec(i) for i in range(pages)],
        out_specs=tok(),
        scratch_shapes=[pltpu.VMEM((2 * A_HEADS, 1), F32), pltpu.VMEM((2 * A_HEADS, 1), F32),
                        pltpu.VMEM((2 * A_HEADS, A_HEAD_DIM), F32)],
    )
    return pl.pallas_call(
        functools.partial(_decode_attn_kernel, pages=pages, n_pages=n_pages, lam_init=lam_init),
        out_shape=jax.ShapeDtypeStruct((bd, A_HEADS, A_HEAD_DIM), F32),
        grid_spec=grid_spec,
        compiler_params=_cparams(("arbitrary", "arbitrary")),
        name="diff_attn_sample",
    )(pt, q, k_new, v_new, lam_params, *([cache_k] * pages), *([cache_v] * pages))


def _ret_step_kernel(q_ref, k_ref, v_ref, st_ref, o_ref, ns_ref, *, nb):
    tn = (((0,), (0,)), ((), ()))
    for i in range(nb):
        for h in range(R_HEADS):
            gamma = RET_GAMMAS[h]
            lgh = math.log(gamma)
            g1 = jnp.exp(jnp.full((1, 1), lgh, F32))
            q = q_ref[i, h:h + 1, :]
            k = k_ref[i, h:h + 1, :] * (R_KEY_DIM ** -0.5)
            v = v_ref[i, h:h + 1, :]
            st = st_ref[i, h]
            qb, kb, vb = q.astype(BF16), k.astype(BF16), v.astype(BF16)
            inner = jnp.sum(qb.astype(F32) * kb.astype(F32), axis=-1, keepdims=True)
            o = (inner.astype(BF16).astype(F32) * vb.astype(F32)
                 + jnp.dot(qb, st.astype(BF16), preferred_element_type=F32) * g1)
            o_ref[i, h:h + 1, :] = o
            k8 = jnp.concatenate([kb, jnp.zeros((7, R_KEY_DIM), BF16)], axis=0)
            v8 = jnp.concatenate([vb, jnp.zeros((7, R_VAL_DIM), BF16)], axis=0)
            ns_ref[i, h] = st * g1 + lax.dot_general(k8, v8, tn, preferred_element_type=F32)


def retention_step(rq, rk, rv, state):
    bd = rq.shape[0]
    nb = 8 if bd % 8 == 0 else 1
    tok = lambda: pl.BlockSpec((nb, R_HEADS, R_VAL_DIM), lambda b: (b, 0, 0))
    st = lambda: pl.BlockSpec((nb, R_HEADS, R_KEY_DIM, R_VAL_DIM), lambda b: (b, 0, 0, 0))
    return pl.pallas_call(
        functools.partial(_ret_step_kernel, nb=nb),
        out_shape=(jax.ShapeDtypeStruct((bd, R_HEADS, R_VAL_DIM), F32),
                   jax.ShapeDtypeStruct((bd, R_HEADS, R_KEY_DIM, R_VAL_DIM), F32)),
        grid=(bd // nb,),
        in_specs=[tok(), tok(), tok(), st()],
        out_specs=(tok(), st()),
        compiler_params=_cparams(("arbitrary",)),
        name="retention_step",
    )(rq, rk, rv, state)


def _prep_params(l, w_in, lambda_q1, lambda_k1, lambda_q2, lambda_k2, a_norm_g, r_norm_g, r_norm_b, w_out,
                 ln1_g, ln1_b, w_router, b_router, w_gate_up, b_gate_up, w_down, b_down, ln2_g, ln2_b):
    wgu = w_gate_up[l]
    bgu = b_gate_up[l]
    return dict(
        w_in=w_in[l].astype(BF16),
        lam=jnp.stack([lambda_q1[l], lambda_k1[l], lambda_q2[l], lambda_k2[l]]),
        a_norm_g=a_norm_g[l].reshape(1, A_HEAD_DIM), r_norm_g=r_norm_g[l], r_norm_b=r_norm_b[l],
        w_out=w_out[l].astype(BF16),
        ln1_g=ln1_g[l].reshape(1, D_MODEL), ln1_b=ln1_b[l].reshape(1, D_MODEL),
        w_router_hi=w_router[l].astype(BF16),
        w_router_lo=(w_router[l] - w_router[l].astype(BF16).astype(F32)).astype(BF16),
        b_router=b_router[l].reshape(1, N_EXPERTS),
        w_gu=jnp.concatenate([wgu[..., 0::2], wgu[..., 1::2]], axis=-1).astype(BF16),
        b_gu=jnp.concatenate([bgu[..., 0::2], bgu[..., 1::2]], axis=-1).reshape(N_EXPERTS, 1, 2 * D_FF),
        w_dn=w_down[l].astype(BF16), b_dn=b_down[l].reshape(N_EXPERTS, 1, D_MODEL),
        ln2_g=ln2_g[l].reshape(1, D_MODEL), ln2_b=ln2_b[l].reshape(1, D_MODEL),
    )


def kernel(x_prompt, x_sample, c_prompt, c_sample, cache_k, cache_v, state_ret, page_table, w_ada, b_ada, w_in, lambda_q1, lambda_k1, lambda_q2, lambda_k2, a_norm_g, r_norm_g, r_norm_b, w_out, ln1_g, ln1_b, w_router, b_router, w_gate_up, b_gate_up, w_down, b_down, ln2_g, ln2_b):
    b, s, _ = x_prompt.shape
    bd = x_sample.shape[0]
    n_phys = cache_k.shape[1]
    xp = x_prompt
    xs = x_sample.reshape(bd, D_MODEL)
    outs = [[] for _ in range(6)]
    for l in range(DEPTH):
        lam_init = 0.8 - 0.6 * math.exp(-0.3 * l)
        p = _prep_params(l, w_in, lambda_q1, lambda_k1, lambda_q2, lambda_k2, a_norm_g, r_norm_g, r_norm_b,
                         w_out, ln1_g, ln1_b, w_router, b_router, w_gate_up, b_gate_up, w_down, b_down,
                         ln2_g, ln2_b)
        mod = ada_modulation(jnp.concatenate([c_prompt, c_sample], axis=0), w_ada[l], b_ada[l])
        mod = mod.reshape(b + bd, 6, D_MODEL)
        mod_p, mod_s = mod[:b], mod[b:]

        aq, ak, av, akb, avb, rq, rk, rv, rg = inproj_prompt(xp, mod_p, p["w_in"])
        a_o = diff_attn_prompt(aq, akb, avb, p["lam"], lam_init)
        r_o, st_p = retention_prompt(rq, rk, rv)
        ts = min(256, s)
        x1, h2, idx, w, rank, cnt = merge_router(a_o, r_o, rg, xp, mod_p, p, lam_init, ts, False)
        xp = moe_block(x1.reshape(b * s, D_MODEL), h2, idx, w, rank, cnt, mod_p, p, ts, s // ts,
                       False).reshape(b, s, D_MODEL)
        outs[0].append(ak.reshape(b, s, A_HEADS, A_HEAD_DIM))
        outs[1].append(av.reshape(b, s, A_HEADS, A_HEAD_DIM))
        outs[2].append(st_p)

        hs = inproj_rows(xs, mod_s, p["w_in"])
        sp = [hs[:, c * A_WIDTH:(c + 1) * A_WIDTH] for c in range(7)]
        heads = lambda z: z.reshape(bd, A_HEADS, A_HEAD_DIM)
        ck = cache_k.reshape(DEPTH * n_phys, PAGE_SIZE * A_HEADS, A_HEAD_DIM)
        cv = cache_v.reshape(DEPTH * n_phys, PAGE_SIZE * A_HEADS, A_HEAD_DIM)
        a_s = diff_attn_sample(heads(sp[0]), heads(sp[1]), heads(sp[2]), ck, cv, page_table + l * n_phys,
                               p["lam"], lam_init)
        r_s, st_s = retention_step(heads(sp[3]), heads(sp[4]), heads(sp[5]), state_ret[l])
        x1s, h2s, idxs, ws, ranks, cnts = merge_router(
            a_s.reshape(bd, A_WIDTH), r_s.reshape(bd, R_WIDTH), sp[6], xs, mod_s, p, lam_init, bd, True)
        xs = moe_block(x1s, h2s, idxs, ws, ranks, cnts, mod_s, p, bd, 1, True)
        outs[3].append(sp[1].reshape(bd, 1, A_HEADS, A_HEAD_DIM))
        outs[4].append(sp[2].reshape(bd, 1, A_HEADS, A_HEAD_DIM))
        outs[5].append(st_s)

    return (xp, xs.reshape(bd, 1, D_MODEL), jnp.stack(outs[0]), jnp.stack(outs[1]), jnp.stack(outs[2]),
            jnp.stack(outs[3]), jnp.stack(outs[4]), jnp.stack(outs[5]))
```

```python
import functools
import math

import jax
import jax.numpy as jnp
from jax import lax
from jax.experimental import pallas as pl
from jax.experimental.pallas import tpu as pltpu

F32 = jnp.float32
BF16 = jnp.bfloat16
I32 = jnp.int32

D_MODEL = 1024
DEPTH = 1
PAGE_SIZE = 128
A_HEADS = 4
A_QK_DIM = 64
A_HEAD_DIM = 128
A_WIDTH = A_HEADS * A_HEAD_DIM
R_HEADS = 4
R_KEY_DIM = 128
R_VAL_DIM = 128
R_WIDTH = R_HEADS * R_VAL_DIM
IN_WIDTH = 3 * A_WIDTH + 2 * R_HEADS * R_KEY_DIM + 2 * R_WIDTH
RET_CHUNK = 128
ALIBI_SLOPES = tuple(2.0 ** (-8.0 * (h + 1) / A_HEADS) for h in range(A_HEADS))
RET_GAMMAS = tuple(1.0 - 2.0 ** (-5.0 - h) for h in range(R_HEADS))
N_EXPERTS = 32
TOP_K = 4
D_FF = D_MODEL
SWIGLU_LIMIT = 7.0
SWIGLU_ALPHA = 1.702
LN_EPS = 1e-5
MASK_VALUE = -1e30
DN_ALPHA = (2.0 * DEPTH) ** 0.25

LANES = 128
BF16_SUBLANES = 16
VMEM_LIMIT = 56 * 1024 * 1024

EXPERT_ROWS = 512
WIN = 64
SORT_CHUNK = 1024
COMBINE_CHUNK = 1024
GU_GROUP = 256
MAX_TAIL_GROUPS =(WIN + EXPERT_ROWS) // BF16_SUBLANES


def _cparams(sem, vmem=VMEM_LIMIT, flags=None):
    return pltpu.CompilerParams(dimension_semantics=sem, vmem_limit_bytes=vmem, flags=flags)


def _ada_kernel(c_ref, w_ref, b_ref, o_ref):
    c = c_ref[...]
    s = (c * jax.nn.sigmoid(c)).astype(BF16)
    o_ref[...] = jnp.dot(s, w_ref[...].astype(BF16), preferred_element_type=F32) + b_ref[...]


def ada_modulation(c, w_ada, b_ada):
    rows = c.shape[0]
    n = w_ada.shape[1]
    tn = 1024
    return pl.pallas_call(
        _ada_kernel,
        out_shape=jax.ShapeDtypeStruct((rows, n), F32),
        grid=(n // tn,),
        in_specs=[pl.BlockSpec((rows, D_MODEL), lambda j: (0, 0)),
                  pl.BlockSpec((D_MODEL, tn), lambda j: (0, j)),
                  pl.BlockSpec((1, tn), lambda j: (0, j))],
        out_specs=pl.BlockSpec((rows, tn), lambda j: (0, j)),
        compiler_params=_cparams(("arbitrary",)),
        name="ada_modulation",
    )(c, w_ada, b_ada.reshape(1, n))


def _inproj_kernel(x_ref, mod_ref, w_ref, aq_ref, ak_ref, av_ref, akb_ref, avb_ref,
                   rq_ref, rk_ref, rv_ref, rg_ref):
    x = x_ref[0]
    sh1 = mod_ref[0, 0:1, :]
    sc1 = mod_ref[0, 1:2, :]
    u = (x * (1.0 + sc1) + sh1).astype(BF16)

    def proj(c):
        return jnp.dot(u, w_ref[:, c * A_WIDTH:(c + 1) * A_WIDTH], preferred_element_type=F32)

    ts = x.shape[0]
    aq_ref[0] = (proj(0) * (A_QK_DIM ** -0.5)).astype(BF16)
    k = proj(1)
    akb_ref[0] = k.astype(BF16)
    v = proj(2)
    avb_ref[0] = v.astype(BF16)
    for h in range(A_HEADS):
        ak_ref[0, pl.ds(h, ts, stride=A_HEADS), :] = k[:, h * A_HEAD_DIM:(h + 1) * A_HEAD_DIM]
        av_ref[0, pl.ds(h, ts, stride=A_HEADS), :] = v[:, h * A_HEAD_DIM:(h + 1) * A_HEAD_DIM]
    rq_ref[0] = proj(3).astype(BF16)
    rk_ref[0] = (proj(4) * (R_KEY_DIM ** -0.5)).astype(BF16)
    rv_ref[0] = proj(5).astype(BF16)
    rg_ref[0] = proj(6).astype(BF16)


def inproj_prompt(x, mod, w_in_bf16):
    b, s, _ = x.shape
    ts = min(512, s)
    blk = lambda: pl.BlockSpec((1, ts, A_WIDTH), lambda i, j: (i, j, 0))
    kvblk = lambda: pl.BlockSpec((1, ts * A_HEADS, A_HEAD_DIM), lambda i, j: (i, j, 0))
    f32o = jax.ShapeDtypeStruct((b, s * A_HEADS, A_HEAD_DIM), F32)
    b16o = jax.ShapeDtypeStruct((b, s, A_WIDTH), BF16)
    return pl.pallas_call(
        _inproj_kernel,
        out_shape=(b16o, f32o, f32o, b16o, b16o, b16o, b16o, b16o, b16o),
        grid=(b, s // ts),
        in_specs=[pl.BlockSpec((1, ts, D_MODEL), lambda i, j: (i, j, 0)),
                  pl.BlockSpec((1, 6, D_MODEL), lambda i, j: (i, 0, 0)),
                  pl.BlockSpec((D_MODEL, IN_WIDTH), lambda i, j: (0, 0))],
        out_specs=(blk(), kvblk(), kvblk()) + tuple(blk() for _ in range(6)),
        compiler_params=_cparams(("arbitrary", "arbitrary")),
        name="inproj_prompt",
    )(x, mod, w_in_bf16)


def _diff_lambda(lam_ref, lam_init):
    lp = lam_ref[...]
    e1 = jnp.exp(jnp.sum(lp[0:1, :] * lp[1:2, :], axis=-1, keepdims=True))
    e2 = jnp.exp(jnp.sum(lp[2:3, :] * lp[3:4, :], axis=-1, keepdims=True))
    return e1 - e2 + lam_init


def _attn_kernel(slopes_ref, q_ref, k_ref, v_ref, lam_ref, o_ref, bias_sc, v1_sc, *, tq, tk, lam_init):
    h = pl.program_id(1)
    i = pl.program_id(2)
    slope = slopes_ref[h]
    s_len = k_ref.shape[1]
    row = lax.broadcasted_iota(I32, (tq, tk), 0)
    col = lax.broadcasted_iota(I32, (tq, tk), 1)

    @pl.when(i == 0)
    def _():
        bias_sc[...] = -slope * (row - col).astype(F32)
        v1_sc[:, :A_HEAD_DIM] = v_ref[0]
        v1_sc[:, A_HEAD_DIM:] = jnp.ones((s_len, A_HEAD_DIM), BF16)

    q = q_ref[0]
    lane = lax.broadcasted_iota(I32, (tq, A_HEAD_DIM), 1)
    zero = jnp.zeros_like(q)
    q1 = jnp.where(lane < A_QK_DIM, q, zero)
    q2 = jnp.where(lane >= A_QK_DIM, q, zero)
    nt = (((1,), (1,)), ((), ()))

    def step(qm, kb, vb, bias, cj, m, a):
        s = lax.dot_general(qm, kb, nt, preferred_element_type=F32) + bias
        m_new = jnp.maximum(m, jnp.max(s, axis=-1, keepdims=True) + cj)
        p = jnp.exp(s - (m_new - cj))
        a = a * jnp.exp(m - m_new) + jnp.dot(p.astype(BF16), vb, preferred_element_type=F32)
        return m_new, a

    mi = jnp.full((tq, 1), MASK_VALUE, F32)
    ai = jnp.zeros((tq, 2 * A_HEAD_DIM), F32)
    lam = _diff_lambda(lam_ref, lam_init)

    def run(qi):
        m1, a1, m2, a2 = mi, ai, mi, ai
        for j in range(qi * (tq // tk)):
            kb = k_ref[0, j * tk:(j + 1) * tk, :]
            vb = v1_sc[j * tk:(j + 1) * tk, :]
            cj = slope * float(j * tk - qi * tq)
            bias = bias_sc[...]
            m1, a1 = step(q1, kb, vb, bias, cj, m1, a1)
            m2, a2 = step(q2, kb, vb, bias, cj, m2, a2)
        for d in range(tq // tk):
            lo = qi * tq + d * tk
            kb = k_ref[0, lo:lo + tk, :]
            vb = v1_sc[lo:lo + tk, :]
            bias = jnp.where(col + d * tk <= row, bias_sc[...], MASK_VALUE)
            cj = slope * float(d * tk)
            m1, a1 = step(q1, kb, vb, bias, cj, m1, a1)
            m2, a2 = step(q2, kb, vb, bias, cj, m2, a2)
        o1 = a1[:, :A_HEAD_DIM] / a1[:, A_HEAD_DIM:A_HEAD_DIM + 1]
        o2 = a2[:, :A_HEAD_DIM] / a2[:, A_HEAD_DIM:A_HEAD_DIM + 1]
        o_ref[0] = (o1 - lam * o2).astype(o_ref.dtype)

    for qi in range(s_len // tq):
        pl.when(i == qi)(functools.partial(run, qi))


def diff_attn_prompt(aq, akb, avb, lam_params, lam_init):
    b, s, _ = aq.shape
    tq = min(512, s)
    tk = tq
    slopes = jnp.asarray(ALIBI_SLOPES, F32)
    grid_spec = pltpu.PrefetchScalarGridSpec(
        num_scalar_prefetch=1,
        grid=(b, A_HEADS, s // tq),
        in_specs=[pl.BlockSpec((1, tq, A_HEAD_DIM), lambda bi, h, i, sl: (bi, i, h)),
                  pl.BlockSpec((1, s, A_HEAD_DIM), lambda bi, h, i, sl: (bi, 0, h)),
                  pl.BlockSpec((1, s, A_HEAD_DIM), lambda bi, h, i, sl: (bi, 0, h)),
                  pl.BlockSpec((4, A_QK_DIM), lambda bi, h, i, sl: (0, 0))],
        out_specs=pl.BlockSpec((1, tq, A_HEAD_DIM), lambda bi, h, i, sl: (bi, i, h)),
        scratch_shapes=[pltpu.VMEM((tq, tk), F32), pltpu.VMEM((s, 2 * A_HEAD_DIM), BF16)],
    )
    return pl.pallas_call(
        functools.partial(_attn_kernel, tq=tq, tk=tk, lam_init=lam_init),
        out_shape=jax.ShapeDtypeStruct((b, s, A_WIDTH), BF16),
        grid_spec=grid_spec,
        compiler_params=_cparams(("arbitrary", "arbitrary", "arbitrary")),
        name="diff_attn_prompt",
    )(slopes, aq, akb, avb, lam_params)


def _ret_kernel(lg_ref, q_ref, k_ref, v_ref, o_ref, st_ref, *, n_chunks):
    h = pl.program_id(1)
    lg = lg_ref[h]
    c = RET_CHUNK
    row = lax.broadcasted_iota(I32, (c, c), 0)
    col = lax.broadcasted_iota(I32, (c, c), 1)
    rel = (row - col).astype(F32)
    decay = jnp.where(rel >= 0, jnp.exp(lg * jnp.maximum(rel, 0.0)), 0.0)
    rowf = lax.broadcasted_iota(I32, (c, R_VAL_DIM), 0).astype(F32)
    q_dec = jnp.exp(lg * (rowf + 1.0))
    k_dec = jnp.exp(lg * (c - 1.0 - rowf))
    lgc = jnp.full((1, 1), c, F32) * lg
    s_dec = jnp.exp(lgc)
    nt = (((1,), (1,)), ((), ()))
    tn = (((0,), (0,)), ((), ()))
    state = jnp.zeros((R_KEY_DIM, R_VAL_DIM), F32)
    for n in range(n_chunks):
        q = q_ref[0, n * c:(n + 1) * c, :]
        k = k_ref[0, n * c:(n + 1) * c, :]
        v = v_ref[0, n * c:(n + 1) * c, :]
        inner = lax.dot_general(q, k, nt, preferred_element_type=F32) * decay
        o = (jnp.dot(inner.astype(BF16), v, preferred_element_type=F32)
             + jnp.dot(q, state.astype(BF16), preferred_element_type=F32) * q_dec)
        o_ref[0, n * c:(n + 1) * c, :] = o.astype(o_ref.dtype)
        kd = (k.astype(F32) * k_dec).astype(BF16)
        state = state * s_dec + lax.dot_general(kd, v, tn, preferred_element_type=F32)
    st_ref[0, 0] = state


def retention_prompt(rq, rk, rv):
    b, s, _ = rq.shape
    lg = jnp.asarray([math.log(g) for g in RET_GAMMAS], F32)
    blk = lambda: pl.BlockSpec((1, s, R_VAL_DIM), lambda bi, h, lgr: (bi, 0, h))
    grid_spec = pltpu.PrefetchScalarGridSpec(
        num_scalar_prefetch=1,
        grid=(b, R_HEADS),
        in_specs=[blk(), blk(), blk()],
        out_specs=(blk(), pl.BlockSpec((1, 1, R_KEY_DIM, R_VAL_DIM), lambda bi, h, lgr: (bi, h, 0, 0))),
    )
    return pl.pallas_call(
        functools.partial(_ret_kernel, n_chunks=s // RET_CHUNK),
        out_shape=(jax.ShapeDtypeStruct((b, s, R_WIDTH), BF16),
                   jax.ShapeDtypeStruct((b, R_HEADS, R_KEY_DIM, R_VAL_DIM), F32)),
        grid_spec=grid_spec,
        compiler_params=_cparams(("arbitrary", "arbitrary")),
        name="retention_prompt",
    )(lg, rq, rk, rv)


def _merge_kernel(ao_ref, ro_ref, rg_ref, x_ref, mod_ref, ag_ref, rgn_ref, rbn_ref, wout_ref,
                  g1_ref, b1_ref, wrh_ref, wrl_ref, br_ref,
                  x1_ref, h2_ref, idx_ref, w_ref, rank_ref, cnt_ref, *, ts, lam_init, per_row_mod):
    if per_row_mod:
        ao, ro, rg, x = ao_ref[...], ro_ref[...], rg_ref[...], x_ref[...]
        gate1, shift2, scale2 = mod_ref[:, 2, :], mod_ref[:, 3, :], mod_ref[:, 4, :]
    else:
        ao, ro, rg, x = ao_ref[0], ro_ref[0], rg_ref[0], x_ref[0]
        gate1, shift2, scale2 = mod_ref[0, 2:3, :], mod_ref[0, 3:4, :], mod_ref[0, 4:5, :]
    ao = ao.astype(F32)
    ro = ro.astype(F32)
    rg = rg.astype(F32)
    pieces = []
    for h in range(A_HEADS):
        a = ao[:, h * A_HEAD_DIM:(h + 1) * A_HEAD_DIM]
        ms = jnp.mean(a * a, axis=-1, keepdims=True)
        pieces.append((a * lax.rsqrt(ms + LN_EPS) * ag_ref[...] * (1.0 - lam_init)).astype(BF16))
    for h in range(R_HEADS):
        r = ro[:, h * R_VAL_DIM:(h + 1) * R_VAL_DIM]
        mu = jnp.mean(r, axis=-1, keepdims=True)
        d = r - mu
        var = jnp.mean(d * d, axis=-1, keepdims=True)
        rn = d * lax.rsqrt(var + LN_EPS) * rgn_ref[h:h + 1, :] + rbn_ref[h:h + 1, :]
        g = rg[:, h * R_VAL_DIM:(h + 1) * R_VAL_DIM]
        pieces.append(((g * jax.nn.sigmoid(g)) * rn).astype(BF16))
    cat = jnp.concatenate(pieces, axis=-1)
    mix = jnp.dot(cat, wout_ref[...], preferred_element_type=F32)
    y = DN_ALPHA * x + (1.0 + gate1) * mix
    mu = jnp.mean(y, axis=-1, keepdims=True)
    d = y - mu
    var = jnp.mean(d * d, axis=-1, keepdims=True)
    x1 = d * lax.rsqrt(var + LN_EPS) * g1_ref[...] + b1_ref[...]
    h2 = x1 * (1.0 + scale2) + shift2
    if per_row_mod:
        x1_ref[...] = x1
    else:
        x1_ref[0] = x1
    h_hi = h2.astype(BF16)
    h2_ref[...] = h_hi
    h_lo = (h2 - h_hi.astype(F32)).astype(BF16)
    logits = (jnp.dot(h_hi, wrh_ref[...], preferred_element_type=F32)
              + jnp.dot(h_lo, wrh_ref[...], preferred_element_type=F32)
              + jnp.dot(h_hi, wrl_ref[...], preferred_element_type=F32)) + br_ref[...]
    lane = lax.broadcasted_iota(I32, (ts, N_EXPERTS), 1).astype(F32)
    lane4 = lax.broadcasted_iota(I32, (ts, TOP_K), 1)
    vals = logits
    member = jnp.zeros((ts, N_EXPERTS), F32)
    sels, tops = [], []
    idx_out = jnp.zeros((ts, TOP_K), I32)
    for k in range(TOP_K):
        m = jnp.max(vals, axis=-1, keepdims=True)
        first = jnp.min(jnp.where(vals == m, lane, float(N_EXPERTS)), axis=-1, keepdims=True)
        sel = lane == first
        sels.append(sel)
        tops.append(m)
        member = member + sel.astype(F32)
        vals = jnp.where(sel, -jnp.inf, vals)
        idx_out = jnp.where(lane4 == k, first.astype(I32), idx_out)
    es = [jnp.exp(t - tops[0]) for t in tops]
    den = es[0] + es[1] + es[2] + es[3]
    w_out = jnp.zeros((ts, TOP_K), F32)
    for k in range(TOP_K):
        w_out = jnp.where(lane4 == k, es[k] / den, w_out)
    tril = (lax.broadcasted_iota(I32, (ts, ts), 1) < lax.broadcasted_iota(I32, (ts, ts), 0))
    prefix = jnp.dot(tril.astype(BF16), member.astype(BF16), preferred_element_type=F32)
    rank_out = jnp.zeros((ts, TOP_K), I32)
    for k in range(TOP_K):
        rk = jnp.sum(jnp.where(sels[k], prefix, 0.0), axis=-1, keepdims=True)
        rank_out = jnp.where(lane4 == k, rk.astype(I32), rank_out)
    idx_ref[...] = idx_out
    w_ref[...] = w_out
    rank_ref[...] = rank_out
    cnt_ref[0] = jnp.sum(member, axis=0, keepdims=True).astype(I32)


def merge_router(ao, ro, rg, x, mod, p, lam_init, ts, per_row_mod):
    if per_row_mod:
        t = x.shape[0]
        grid = (t // ts,)
        tok = lambda w: pl.BlockSpec((ts, w), lambda j: (j, 0))
        in_tok = [tok(A_WIDTH), tok(R_WIDTH), tok(R_WIDTH), tok(D_MODEL),
                  pl.BlockSpec((ts, 6, D_MODEL), lambda j: (j, 0, 0))]
        x1_spec = tok(D_MODEL)
        x1_shape = jax.ShapeDtypeStruct((t, D_MODEL), F32)
        flat = lambda j: j
        const = lambda shape: pl.BlockSpec(shape, lambda j: (0,) * len(shape))
        sem = ("arbitrary",)
    else:
        b, s, _ = x.shape
        t = b * s
        nts = s // ts
        grid = (b, nts)
        tok = lambda w: pl.BlockSpec((1, ts, w), lambda i, j: (i, j, 0))
        in_tok = [tok(A_WIDTH), tok(R_WIDTH), tok(R_WIDTH), tok(D_MODEL),
                  pl.BlockSpec((1, 6, D_MODEL), lambda i, j: (i, 0, 0))]
        x1_spec = tok(D_MODEL)
        x1_shape = jax.ShapeDtypeStruct((b, s, D_MODEL), F32)
        flat = lambda i, j: i * nts + j
        const = lambda shape: pl.BlockSpec(shape, lambda i, j: (0,) * len(shape))
        sem = ("arbitrary", "arbitrary")
    nt = t // ts
    small = lambda: pl.BlockSpec((ts, TOP_K), lambda *a: (flat(*a), 0))
    return pl.pallas_call(
        functools.partial(_merge_kernel, ts=ts, lam_init=lam_init, per_row_mod=per_row_mod),
        out_shape=(x1_shape,
                   jax.ShapeDtypeStruct((t, D_MODEL), BF16),
                   jax.ShapeDtypeStruct((t, TOP_K), I32),
                   jax.ShapeDtypeStruct((t, TOP_K), F32),
                   jax.ShapeDtypeStruct((t, TOP_K), I32),
                   jax.ShapeDtypeStruct((nt, 1, N_EXPERTS), I32)),
        grid=grid,
        in_specs=in_tok + [const((1, A_HEAD_DIM)), const((R_HEADS, R_VAL_DIM)), const((R_HEADS, R_VAL_DIM)),
                           const((D_MODEL, D_MODEL)), const((1, D_MODEL)), const((1, D_MODEL)),
                           const((D_MODEL, N_EXPERTS)), const((D_MODEL, N_EXPERTS)), const((1, N_EXPERTS))],
        out_specs=(x1_spec,
                   pl.BlockSpec((ts, D_MODEL), lambda *a: (flat(*a), 0)),
                   small(), small(), small(),
                   pl.BlockSpec((1, 1, N_EXPERTS), lambda *a: (flat(*a), 0, 0))),
        compiler_params=_cparams(sem),
        name="merge_router_rows" if per_row_mod else "merge_router",
    )(ao, ro, rg, x, mod, p["a_norm_g"], p["r_norm_g"], p["r_norm_b"], p["w_out"],
      p["ln1_g"], p["ln1_b"], p["w_router_hi"], p["w_router_lo"], p["b_router"])


def _round_up(x, m):
    return (x + m - 1) // m * m


def _max_windows(ts):
    return N_EXPERTS + (TOP_K * ts + N_EXPERTS * (BF16_SUBLANES - 1) + WIN - 1) // WIN


def _local_rows(ts):
    return _round_up(TOP_K * ts + N_EXPERTS * (WIN - 1), COMBINE_CHUNK)


def _num_blocks(t, ts):
    nt = t // ts
    rows = TOP_K * t + nt * N_EXPERTS * (BF16_SUBLANES - 1) + N_EXPERTS * (WIN + EXPERT_ROWS - 1)
    return rows // EXPERT_ROWS + 1


def routing_plan(cnt, ts, n_blocks):
    nt = cnt.shape[0]
    maxw = _max_windows(ts)
    c16 = _round_up(cnt, BF16_SUBLANES)
    nwin = (c16 + WIN - 1) // WIN
    l64 = nwin * WIN
    loff = jnp.cumsum(l64, axis=1) - l64
    n_chunk = (jnp.sum(l64, axis=1) + SORT_CHUNK - 1) // SORT_CHUNK
    tot = jnp.sum(c16, axis=0)
    region = _round_up(tot + WIN, EXPERT_ROWS)
    rend = jnp.cumsum(region)
    goff = (rend - region)[None, :] + jnp.cumsum(c16, axis=0) - c16
    n_used = (rend[-1] // EXPERT_ROWS).astype(I32)
    blk = jnp.arange(n_blocks, dtype=I32)
    block_e = jnp.minimum(jnp.sum((rend // EXPERT_ROWS)[None, :] <= blk[:, None], axis=1), N_EXPERTS - 1).astype(I32)
    cw = jnp.cumsum(nwin, axis=1)
    wi = jnp.arange(maxw, dtype=I32)
    we = jnp.minimum(jnp.sum(cw[:, None, :] <= wi[None, :, None], axis=2), N_EXPERTS - 1)
    wk = wi[None, :] - jnp.take_along_axis(cw - nwin, we, axis=1)
    win_src = jnp.take_along_axis(loff, we, axis=1) + wk * WIN
    win_dst = jnp.take_along_axis(goff, we, axis=1) + wk * WIN
    valid = wi[None, :] < cw[:, -1:]
    win_src = jnp.where(valid, win_src, 0).astype(I32).reshape(-1)
    win_dst = jnp.where(valid, win_dst, 0).astype(I32).reshape(-1)
    tail_n = (region - tot) // BF16_SUBLANES
    zc = jnp.cumsum(tail_n)
    zi = jnp.arange(N_EXPERTS * MAX_TAIL_GROUPS, dtype=I32)
    ze = jnp.minimum(jnp.sum(zc[None, :] <= zi[:, None], axis=1), N_EXPERTS - 1)
    zdst = (rend - region + tot)[ze] + (zi - (zc - tail_n)[ze]) * BF16_SUBLANES
    zdst = jnp.where(zi < zc[-1], zdst, 0).astype(I32)
    return dict(n_win=cw[:, -1].astype(I32), win_src=win_src, win_dst=win_dst, n_chunk=n_chunk.astype(I32),
                loff=loff.astype(I32), block_e=block_e, n_used=n_used.reshape(1),
                n_zero=zc[-1].astype(I32).reshape(1), zero_dst=zdst)


def _wait_copies(n, src, dst, sem, rows=WIN):
    def body(i, c):
        pltpu.make_async_copy(src.at[pl.ds(0, rows)], dst.at[pl.ds(0, rows)], sem).wait()
        return c
    lax.fori_loop(0, n, body, 0)


def _dispatch_kernel(nwin_ref, src_ref, dst_ref, nchunk_ref, nzero_ref, zdst_ref, nused_ref,
                     h2_ref, idxt_ref, rankt_ref, loff_ref, xbuf_ref, sbuf, zbuf, sem, zsem,
                     *, ts, maxw, n_blocks):
    j = pl.program_id(0)
    slot = j % 2
    cur = sbuf.at[slot]

    @pl.when(j == 0)
    def _():
        sbuf[...] = jnp.zeros_like(sbuf)
        zbuf[...] = jnp.zeros_like(zbuf)
        nz = nzero_ref[0]
        nu = nused_ref[0]

        def tail(i, c):
            d = pl.multiple_of(zdst_ref[i], BF16_SUBLANES)
            pltpu.make_async_copy(zbuf.at[pl.ds(0, BF16_SUBLANES)], xbuf_ref.at[pl.ds(d, BF16_SUBLANES)],
                                  zsem.at[0]).start()
            return c

        def block(b, c):
            d = pl.multiple_of(b * EXPERT_ROWS, EXPERT_ROWS)
            pltpu.make_async_copy(zbuf, xbuf_ref.at[pl.ds(d, EXPERT_ROWS)], zsem.at[1]).start()
            return c

        lax.fori_loop(0, nz, tail, 0)
        lax.fori_loop(nu, n_blocks, block, 0)
        _wait_copies(nz, zbuf, xbuf_ref, zsem.at[0], BF16_SUBLANES)
        _wait_copies(n_blocks - nu, zbuf, xbuf_ref, zsem.at[1], EXPERT_ROWS)

    e_iota = lax.broadcasted_iota(I32, (N_EXPERTS, ts), 0)
    loff = loff_ref[0].astype(F32)
    ld = []
    for k in range(TOP_K):
        hit = idxt_ref[k:k + 1, :] == e_iota
        off = jnp.sum(jnp.where(hit, loff, 0.0), axis=0, keepdims=True)
        ld.append(off.astype(I32) + rankt_ref[k:k + 1, :])
    x = h2_ref[...]
    row = lax.broadcasted_iota(I32, (SORT_CHUNK, ts), 0)

    def chunk(c, carry):
        r0 = pl.multiple_of(c * SORT_CHUNK, SORT_CHUNK)
        rows = row + r0
        onehot = (ld[0] == rows) | (ld[1] == rows) | (ld[2] == rows) | (ld[3] == rows)
        p = jnp.where(onehot, 1.0, 0.0).astype(BF16)
        cur[pl.ds(r0, SORT_CHUNK), :] = jnp.dot(p, x, preferred_element_type=F32).astype(BF16)
        return carry

    lax.fori_loop(0, nchunk_ref[j], chunk, 0)

    @pl.when(j > 0)
    def _():
        _wait_copies(nwin_ref[j - 1], sbuf.at[1 - slot], xbuf_ref, sem.at[1 - slot])

    n = nwin_ref[j]

    def issue(i, carry):
        s = pl.multiple_of(src_ref[j * maxw + i], BF16_SUBLANES)
        d = pl.multiple_of(dst_ref[j * maxw + i], BF16_SUBLANES)
        pltpu.make_async_copy(cur.at[pl.ds(s, WIN)], xbuf_ref.at[pl.ds(d, WIN)], sem.at[slot]).start()
        return carry

    lax.fori_loop(0, n, issue, 0)

    @pl.when(j == pl.num_programs(0) - 1)
    def _():
        _wait_copies(n, cur, xbuf_ref, sem.at[slot])


def moe_dispatch(h2, idx_t, rank_t, plan, ts, n_blocks):
    t = h2.shape[0]
    nt = t // ts
    maxw = _max_windows(ts)
    lrows = _local_rows(ts)
    grid_spec = pltpu.PrefetchScalarGridSpec(
        num_scalar_prefetch=7,
        grid=(nt,),
        in_specs=[pl.BlockSpec((ts, D_MODEL), lambda j, *_: (j, 0)),
                  pl.BlockSpec((TOP_K, ts), lambda j, *_: (0, j)),
                  pl.BlockSpec((TOP_K, ts), lambda j, *_: (0, j)),
                  pl.BlockSpec((1, N_EXPERTS, 1), lambda j, *_: (j, 0, 0))],
        out_specs=pl.BlockSpec(memory_space=pl.ANY),
        scratch_shapes=[pltpu.VMEM((2, lrows, D_MODEL), BF16), pltpu.VMEM((EXPERT_ROWS, D_MODEL), BF16),
                        pltpu.SemaphoreType.DMA((2,)), pltpu.SemaphoreType.DMA((2,))],
    )
    return pl.pallas_call(
        functools.partial(_dispatch_kernel, ts=ts, maxw=maxw, n_blocks=n_blocks),
        out_shape=jax.ShapeDtypeStruct((n_blocks * EXPERT_ROWS, D_MODEL), BF16),
        grid_spec=grid_spec,
        compiler_params=_cparams(("arbitrary",)),
        name="moe_dispatch",
    )(plan["n_win"], plan["win_src"], plan["win_dst"], plan["n_chunk"],
      plan["n_zero"], plan["zero_dst"], plan["n_used"],
      h2, idx_t, rank_t, plan["loff"].reshape(nt, N_EXPERTS, 1))


def _gu_regroup_kernel(w_ref, perm_ref, o_ref, *, tn):
    w = w_ref[0].astype(BF16)
    for g in range(tn // GU_GROUP):
        cols = slice(g * GU_GROUP, (g + 1) * GU_GROUP)
        o_ref[0, :, cols] = jnp.dot(w[:, cols], perm_ref[...], preferred_element_type=F32).astype(BF16)


def regroup_gate_up(w_gate_up):
    e, d, n = w_gate_up.shape
    tn = 512
    c = jnp.arange(GU_GROUP)
    perm = (c[None, :] == (c // 2 + (GU_GROUP // 2) * (c % 2))[:, None]).astype(BF16)
    return pl.pallas_call(
        functools.partial(_gu_regroup_kernel, tn=tn),
        out_shape=jax.ShapeDtypeStruct((e, d, n), BF16),
        grid=(e, n // tn),
        in_specs=[pl.BlockSpec((1, d, tn), lambda i, j: (i, 0, j)),
                  pl.BlockSpec((GU_GROUP, GU_GROUP), lambda i, j: (0, 0))],
        out_specs=pl.BlockSpec((1, d, tn), lambda i, j: (i, 0, j)),
        compiler_params=_cparams(("arbitrary", "arbitrary")),
        name="regroup_gate_up",
    )(w_gate_up, perm)


def _expert_kernel(be_ref, nu_ref, x_ref, wg_ref, bg_ref, wd_ref, bd_ref, y_ref):
    b = pl.program_id(0)
    half = GU_GROUP // 2

    @pl.when(b < nu_ref[0])
    def _():
        gu = jnp.dot(x_ref[...], wg_ref[0], preferred_element_type=F32) + bg_ref[0]
        acts = []
        for g in range(2 * D_FF // GU_GROUP):
            glu = jnp.minimum(gu[:, g * GU_GROUP:g * GU_GROUP + half], SWIGLU_LIMIT)
            lin = jnp.clip(gu[:, g * GU_GROUP + half:(g + 1) * GU_GROUP], -SWIGLU_LIMIT, SWIGLU_LIMIT)
            acts.append((glu * jax.nn.sigmoid(SWIGLU_ALPHA * glu) * (lin + 1.0)).astype(BF16))
        act = jnp.concatenate(acts, axis=1)
        y = jnp.dot(act, wd_ref[0], preferred_element_type=F32) + bd_ref[0]
        y_ref[...] = y.astype(BF16)

    @pl.when(b >= nu_ref[0])
    def _():
        y_ref[...] = jnp.zeros_like(y_ref)


def moe_experts(xbuf, plan, w_gu, b_gu, w_dn, b_dn, n_blocks):
    row_map = lambda b, be, nu: (jnp.minimum(b, nu[0] - 1), 0)
    out_map = lambda b, be, nu: (b, 0)
    e_map3 = lambda b, be, nu: (be[b], 0, 0)
    grid_spec = pltpu.PrefetchScalarGridSpec(
        num_scalar_prefetch=2,
        grid=(n_blocks,),
        in_specs=[pl.BlockSpec((EXPERT_ROWS, D_MODEL), row_map),
                  pl.BlockSpec((1, D_MODEL, 2 * D_FF), e_map3),
                  pl.BlockSpec((1, 1, 2 * D_FF), e_map3),
                  pl.BlockSpec((1, D_FF, D_MODEL), e_map3),
                  pl.BlockSpec((1, 1, D_MODEL), e_map3)],
        out_specs=pl.BlockSpec((EXPERT_ROWS, D_MODEL), out_map),
    )
    return pl.pallas_call(
        _expert_kernel,
        out_shape=jax.ShapeDtypeStruct((n_blocks * EXPERT_ROWS, D_MODEL), BF16),
        grid_spec=grid_spec,
        compiler_params=_cparams(("arbitrary",)),
        name="moe_experts",
    )(plan["block_e"], plan["n_used"], xbuf, w_gu, b_gu, w_dn, b_dn)


def _combine_kernel(nwin_ref, src_ref, dst_ref, nchunk_ref,
                    x1_ref, mod_ref, idx_ref, w_ref, rank_ref, loff_ref, g2_ref, b2_ref, ybuf_ref,
                    o_ref, ywin, sem, *, ts, maxw, per_row_mod):
    j = pl.program_id(0)
    slot = j % 2
    cur = ywin.at[slot]

    def fetch(t, sl):
        def issue(i, carry):
            s = pl.multiple_of(src_ref[t * maxw + i], BF16_SUBLANES)
            d = pl.multiple_of(dst_ref[t * maxw + i], BF16_SUBLANES)
            pltpu.make_async_copy(ybuf_ref.at[pl.ds(d, WIN)], ywin.at[sl].at[pl.ds(s, WIN)], sem.at[sl]).start()
            return carry
        lax.fori_loop(0, nwin_ref[t], issue, 0)

    @pl.when(j == 0)
    def _():
        ywin[...] = jnp.zeros_like(ywin)
        fetch(0, 0)

    @pl.when(j + 1 < pl.num_programs(0))
    def _():
        fetch(j + 1, 1 - slot)

    e_iota = lax.broadcasted_iota(I32, (ts, N_EXPERTS), 1)
    loff = loff_ref[0].astype(F32)
    ld, wk = [], []
    for k in range(TOP_K):
        hit = idx_ref[:, k:k + 1] == e_iota
        off = jnp.sum(jnp.where(hit, loff, 0.0), axis=-1, keepdims=True)
        ldk = off.astype(I32) + rank_ref[:, k:k + 1]
        ld.append(jnp.broadcast_to(ldk, (ts, LANES)))
        wk.append(jnp.broadcast_to(w_ref[:, k:k + 1], (ts, LANES)))
    lane = lax.broadcasted_iota(I32, (ts, LANES), 1)
    _wait_copies(nwin_ref[j], ybuf_ref, cur, sem.at[slot])

    def chunk(c, f):
        r0 = pl.multiple_of(c * COMBINE_CHUNK, COMBINE_CHUNK)
        groups = []
        for g in range(COMBINE_CHUNK // LANES):
            cols = lane + (r0 + g * LANES)
            pw = jnp.zeros((ts, LANES), F32)
            for k in range(TOP_K):
                pw = pw + jnp.where(ld[k] == cols, wk[k], 0.0)
            groups.append(pw.astype(BF16))
        pw = jnp.concatenate(groups, axis=1)
        return f + jnp.dot(pw, cur[pl.ds(r0, COMBINE_CHUNK), :], preferred_element_type=F32)

    n_chunk = (nchunk_ref[j] * SORT_CHUNK + COMBINE_CHUNK - 1) // COMBINE_CHUNK
    f = lax.fori_loop(0, n_chunk, chunk, jnp.zeros((ts, D_MODEL), F32))
    if per_row_mod:
        x1 = x1_ref[...]
        gate2 = mod_ref[:, 5, :]
    else:
        x1 = x1_ref[...]
        gate2 = mod_ref[0, 5:6, :]
    y = DN_ALPHA * x1 + (1.0 + gate2) * f
    mu = jnp.mean(y, axis=-1, keepdims=True)
    d = y - mu
    var = jnp.mean(d * d, axis=-1, keepdims=True)
    o_ref[...] = d * lax.rsqrt(var + LN_EPS) * g2_ref[...] + b2_ref[...]


def moe_combine(x1, mod, idx, w, rank, plan, ybuf, ln2_g, ln2_b, ts, tiles_per_mod, per_row_mod):
    t = x1.shape[0]
    nt = t // ts
    maxw = _max_windows(ts)
    lrows = _local_rows(ts)
    if per_row_mod:
        mod_spec = pl.BlockSpec((ts, 6, D_MODEL), lambda j, *_: (j, 0, 0))
    else:
        mod_spec = pl.BlockSpec((1, 6, D_MODEL), lambda j, *_: (j // tiles_per_mod, 0, 0))
    small = lambda: pl.BlockSpec((ts, TOP_K), lambda j, *_: (j, 0))
    grid_spec = pltpu.PrefetchScalarGridSpec(
        num_scalar_prefetch=4,
        grid=(nt,),
        in_specs=[pl.BlockSpec((ts, D_MODEL), lambda j, *_: (j, 0)),
                  mod_spec, small(), small(), small(),
                  pl.BlockSpec((1, 1, N_EXPERTS), lambda j, *_: (j, 0, 0)),
                  pl.BlockSpec((1, D_MODEL), lambda j, *_: (0, 0)),
                  pl.BlockSpec((1, D_MODEL), lambda j, *_: (0, 0)),
                  pl.BlockSpec(memory_space=pl.ANY)],
        out_specs=pl.BlockSpec((ts, D_MODEL), lambda j, *_: (j, 0)),
        scratch_shapes=[pltpu.VMEM((2, lrows, D_MODEL), BF16), pltpu.SemaphoreType.DMA((2,))],
    )
    return pl.pallas_call(
        functools.partial(_combine_kernel, ts=ts, maxw=maxw, per_row_mod=per_row_mod),
        out_shape=jax.ShapeDtypeStruct((t, D_MODEL), F32),
        grid_spec=grid_spec,
        compiler_params=_cparams(("arbitrary",)),
        name="moe_combine_rows" if per_row_mod else "moe_combine",
    )(plan["n_win"], plan["win_src"], plan["win_dst"], plan["n_chunk"],
      x1, mod, idx, w, rank, plan["loff"].reshape(nt, 1, N_EXPERTS), ln2_g, ln2_b, ybuf)


def moe_block(x1_flat, h2, idx, w, rank, cnt, mod, p, ts, tiles_per_mod, per_row_mod):
    t = h2.shape[0]
    n_blocks = _num_blocks(t, ts)
    plan = routing_plan(cnt.reshape(-1, N_EXPERTS), ts, n_blocks)
    xbuf = moe_dispatch(h2, idx.T, rank.T, plan, ts, n_blocks)
    ybuf = moe_experts(xbuf, plan, p["w_gu"], p["b_gu"], p["w_dn"], p["b_dn"], n_blocks)
    return moe_combine(x1_flat, mod, idx, w, rank, plan, ybuf, p["ln2_g"], p["ln2_b"], ts,
                       tiles_per_mod, per_row_mod)


def _inproj_rows_kernel(x_ref, mod_ref, w_ref, h_ref):
    u = (x_ref[...] * (1.0 + mod_ref[:, 1, :]) + mod_ref[:, 0, :]).astype(BF16)
    h_ref[...] = jnp.dot(u, w_ref[...], preferred_element_type=F32)


def inproj_rows(x, mod, w_in_bf16):
    t = x.shape[0]
    tn = A_WIDTH
    return pl.pallas_call(
        _inproj_rows_kernel,
        out_shape=jax.ShapeDtypeStruct((t, IN_WIDTH), F32),
        grid=(IN_WIDTH // tn,),
        in_specs=[pl.BlockSpec((t, D_MODEL), lambda j: (0, 0)),
                  pl.BlockSpec((t, 6, D_MODEL), lambda j: (0, 0, 0)),
                  pl.BlockSpec((D_MODEL, tn), lambda j: (0, j))],
        out_specs=pl.BlockSpec((t, tn), lambda j: (0, j)),
        compiler_params=_cparams(("arbitrary",)),
        name="inproj_rows",
    )(x, mod, w_in_bf16)


def _decode_attn_kernel(pt_ref, q_ref, kn_ref, vn_ref, lam_ref, *rest, pages, n_pages, lam_init):
    del pt_ref
    k_refs = rest[:pages]
    v_refs = rest[pages:2 * pages]
    o_ref = rest[2 * pages]
    m_sc, l_sc, a_sc = rest[2 * pages + 1:]
    s_id = pl.program_id(1)
    n_steps = pl.num_programs(1)
    rows = PAGE_SIZE * A_HEADS
    r8 = 2 * A_HEADS

    @pl.when(s_id == 0)
    def _():
        m_sc[...] = jnp.full_like(m_sc, MASK_VALUE)
        l_sc[...] = jnp.zeros_like(l_sc)
        a_sc[...] = jnp.zeros_like(a_sc)

    q4 = q_ref[0] * (A_QK_DIM ** -0.5)
    q8 = jnp.concatenate([q4, q4], axis=0)
    rlane = lax.broadcasted_iota(I32, (r8, A_HEAD_DIM), 1)
    rrow = lax.broadcasted_iota(I32, (r8, A_HEAD_DIM), 0)
    keep = ((rrow < A_HEADS) & (rlane < A_QK_DIM)) | ((rrow >= A_HEADS) & (rlane >= A_QK_DIM))
    q8 = jnp.where(keep, q8, 0.0)
    q8b = q8.astype(BF16)

    srow = lax.broadcasted_iota(I32, (r8, rows), 0)
    scol = lax.broadcasted_iota(I32, (r8, rows), 1)
    head_r = srow & (A_HEADS - 1)
    own = (scol & (A_HEADS - 1)) == head_r
    slope = jnp.full((r8, rows), ALIBI_SLOPES[0], F32)
    for hh in range(1, A_HEADS):
        slope = jnp.where(head_r == hh, ALIBI_SLOPES[hh], slope)
    pos = scol >> 2
    q_pos = n_pages * PAGE_SIZE
    nt = (((1,), (1,)), ((), ()))

    scores = []
    for i in range(pages):
        page = s_id * pages + i
        dist = (q_pos - page * PAGE_SIZE - pos).astype(F32)
        kb = k_refs[i][0].astype(BF16)
        s = lax.dot_general(q8b, kb, nt, preferred_element_type=F32)
        scores.append(jnp.where(own, s - slope * dist, MASK_VALUE))
    m_old = m_sc[...]
    m_new = m_old
    for s in scores:
        m_new = jnp.maximum(m_new, jnp.max(s, axis=-1, keepdims=True))
    corr = jnp.exp(m_old - m_new)
    l = l_sc[...] * corr
    acc = a_sc[...] * corr
    for i, s in enumerate(scores):
        p = jnp.exp(s - m_new)
        l = l + jnp.sum(p, axis=-1, keepdims=True)
        acc = acc + jnp.dot(p.astype(BF16), v_refs[i][0].astype(BF16), preferred_element_type=F32)
    m_sc[...] = m_new
    l_sc[...] = l
    a_sc[...] = acc

    @pl.when(s_id == n_steps - 1)
    def _():
        kn = kn_ref[0]
        vn = vn_ref[0]
        k8 = jnp.concatenate([kn, kn], axis=0)
        v8 = jnp.concatenate([vn, vn], axis=0)
        s_new = jnp.sum(q8b.astype(F32) * k8.astype(BF16).astype(F32), axis=-1, keepdims=True)
        m_fin = jnp.maximum(m_new, s_new)
        c2 = jnp.exp(m_new - m_fin)
        p_new = jnp.exp(s_new - m_fin)
        l_fin = l * c2 + p_new
        a_fin = acc * c2 + p_new.astype(BF16).astype(F32) * v8.astype(BF16).astype(F32)
        o = a_fin / l_fin
        lam = _diff_lambda(lam_ref, lam_init)
        o_ref[0] = o[:A_HEADS] - lam * o[A_HEADS:]


def diff_attn_sample(q, k_new, v_new, cache_k, cache_v, page_table, lam_params, lam_init):
    bd, n_pages = page_table.shape
    pages = min(16, n_pages)
    rows = PAGE_SIZE * A_HEADS
    pt = page_table.reshape(-1)

    def page_spec(i):
        return pl.BlockSpec((1, rows, A_HEAD_DIM), lambda b, s, ptr: (ptr[b * n_pages + s * pages + i], 0, 0))

    tok = lambda: pl.BlockSpec((1, A_HEADS, A_HEAD_DIM), lambda b, s, ptr: (b, 0, 0))
    grid_spec = pltpu.PrefetchScalarGridSpec(
        num_scalar_prefetch=1,
        grid=(bd, n_pages // pages),
        in_specs=[tok(), tok(), tok(), pl.BlockSpec((4, A_QK_DIM), lambda b, s, ptr: (0, 0))]
        + [page_spec(i) for i in range(pages)] + [page_spec(i) for i in range(pages)],
        out_specs=tok(),
        scratch_shapes=[pltpu.VMEM((2 * A_HEADS, 1), F32), pltpu.VMEM((2 * A_HEADS, 1), F32),
                        pltpu.VMEM((2 * A_HEADS, A_HEAD_DIM), F32)],
    )
    return pl.pallas_call(
        functools.partial(_decode_attn_kernel, pages=pages, n_pages=n_pages, lam_init=lam_init),
        out_shape=jax.ShapeDtypeStruct((bd, A_HEADS, A_HEAD_DIM), F32),
        grid_spec=grid_spec,
        compiler_params=_cparams(("arbitrary", "arbitrary")),
        name="diff_attn_sample",
    )(pt, q, k_new, v_new, lam_params, *([cache_k] * pages), *([cache_v] * pages))


def _ret_step_kernel(q_ref, k_ref, v_ref, st_ref, o_ref, ns_ref, *, nb):
    tn = (((0,), (0,)), ((), ()))
    for i in range(nb):
        for h in range(R_HEADS):
            gamma = RET_GAMMAS[h]
            lgh = math.log(gamma)
            g1 = jnp.exp(jnp.full((1, 1), lgh, F32))
            q = q_ref[i, h:h + 1, :]
            k = k_ref[i, h:h + 1, :] * (R_KEY_DIM ** -0.5)
            v = v_ref[i, h:h + 1, :]
            st = st_ref[i, h]
            qb, kb, vb = q.astype(BF16), k.astype(BF16), v.astype(BF16)
            inner = jnp.sum(qb.astype(F32) * kb.astype(F32), axis=-1, keepdims=True)
            o = (inner.astype(BF16).astype(F32) * vb.astype(F32)
                 + jnp.dot(qb, st.astype(BF16), preferred_element_type=F32) * g1)
            o_ref[i, h:h + 1, :] = o
            k8 = jnp.concatenate([kb, jnp.zeros((7, R_KEY_DIM), BF16)], axis=0)
            v8 = jnp.concatenate([vb, jnp.zeros((7, R_VAL_DIM), BF16)], axis=0)
            ns_ref[i, h] = st * g1 + lax.dot_general(k8, v8, tn, preferred_element_type=F32)


def retention_step(rq, rk, rv, state):
    bd = rq.shape[0]
    nb = 8 if bd % 8 == 0 else 1
    tok = lambda: pl.BlockSpec((nb, R_HEADS, R_VAL_DIM), lambda b: (b, 0, 0))
    st = lambda: pl.BlockSpec((nb, R_HEADS, R_KEY_DIM, R_VAL_DIM), lambda b: (b, 0, 0, 0))
    return pl.pallas_call(
        functools.partial(_ret_step_kernel, nb=nb),
        out_shape=(jax.ShapeDtypeStruct((bd, R_HEADS, R_VAL_DIM), F32),
                   jax.ShapeDtypeStruct((bd, R_HEADS, R_KEY_DIM, R_VAL_DIM), F32)),
        grid=(bd // nb,),
        in_specs=[tok(), tok(), tok(), st()],
        out_specs=(tok(), st()),
        compiler_params=_cparams(("arbitrary",)),
        name="retention_step",
    )(rq, rk, rv, state)


def _prep_params(l, w_in, lambda_q1, lambda_k1, lambda_q2, lambda_k2, a_norm_g, r_norm_g, r_norm_b, w_out,
                 ln1_g, ln1_b, w_router, b_router, w_gate_up, b_gate_up, w_down, b_down, ln2_g, ln2_b):
    wgu = w_gate_up[l]
    bgu = b_gate_up[l]
    return dict(
        w_in=w_in[l].astype(BF16),
        lam=jnp.stack([lambda_q1[l], lambda_k1[l], lambda_q2[l], lambda_k2[l]]),
        a_norm_g=a_norm_g[l].reshape(1, A_HEAD_DIM), r_norm_g=r_norm_g[l], r_norm_b=r_norm_b[l],
        w_out=w_out[l].astype(BF16),
        ln1_g=ln1_g[l].reshape(1, D_MODEL), ln1_b=ln1_b[l].reshape(1, D_MODEL),
        w_router_hi=w_router[l].astype(BF16),
        w_router_lo=(w_router[l] - w_router[l].astype(BF16).astype(F32)).astype(BF16),
        b_router=b_router[l].reshape(1, N_EXPERTS),
        w_gu=regroup_gate_up(wgu),
        b_gu=bgu.reshape(N_EXPERTS, 2 * D_FF // GU_GROUP, GU_GROUP // 2, 2).transpose(0, 1, 3, 2)
        .reshape(N_EXPERTS, 1, 2 * D_FF),
        w_dn=w_down[l].astype(BF16), b_dn=b_down[l].reshape(N_EXPERTS, 1, D_MODEL),
        ln2_g=ln2_g[l].reshape(1, D_MODEL), ln2_b=ln2_b[l].reshape(1, D_MODEL),
    )


def kernel(x_prompt, x_sample, c_prompt, c_sample, cache_k, cache_v, state_ret, page_table, w_ada, b_ada, w_in, lambda_q1, lambda_k1, lambda_q2, lambda_k2, a_norm_g, r_norm_g, r_norm_b, w_out, ln1_g, ln1_b, w_router, b_router, w_gate_up, b_gate_up, w_down, b_down, ln2_g, ln2_b):
    b, s, _ = x_prompt.shape
    bd = x_sample.shape[0]
    n_phys = cache_k.shape[1]
    xp = x_prompt
    xs = x_sample.reshape(bd, D_MODEL)
    outs = [[] for _ in range(6)]
    for l in range(DEPTH):
        lam_init = 0.8 - 0.6 * math.exp(-0.3 * l)
        p = _prep_params(l, w_in, lambda_q1, lambda_k1, lambda_q2, lambda_k2, a_norm_g, r_norm_g, r_norm_b,
                         w_out, ln1_g, ln1_b, w_router, b_router, w_gate_up, b_gate_up, w_down, b_down,
                         ln2_g, ln2_b)
        mod = ada_modulation(jnp.concatenate([c_prompt, c_sample], axis=0), w_ada[l], b_ada[l])
        mod = mod.reshape(b + bd, 6, D_MODEL)
        mod_p, mod_s = mod[:b], mod[b:]

        aq, ak, av, akb, avb, rq, rk, rv, rg = inproj_prompt(xp, mod_p, p["w_in"])
        a_o = diff_attn_prompt(aq, akb, avb, p["lam"], lam_init)
        r_o, st_p = retention_prompt(rq, rk, rv)
        ts = min(256, s)
        x1, h2, idx, w, rank, cnt = merge_router(a_o, r_o, rg, xp, mod_p, p, lam_init, ts, False)
        xp = moe_block(x1.reshape(b * s, D_MODEL), h2, idx, w, rank, cnt, mod_p, p, ts, s // ts,
                       False).reshape(b, s, D_MODEL)
        outs[0].append(ak.reshape(b, s, A_HEADS, A_HEAD_DIM))
        outs[1].append(av.reshape(b, s, A_HEADS, A_HEAD_DIM))
        outs[2].append(st_p)

        hs = inproj_rows(xs, mod_s, p["w_in"])
        sp = [hs[:, c * A_WIDTH:(c + 1) * A_WIDTH] for c in range(7)]
        heads = lambda z: z.reshape(bd, A_HEADS, A_HEAD_DIM)
        ck = cache_k.reshape(DEPTH * n_phys, PAGE_SIZE * A_HEADS, A_HEAD_DIM)
        cv = cache_v.reshape(DEPTH * n_phys, PAGE_SIZE * A_HEADS, A_HEAD_DIM)
        a_s = diff_attn_sample(heads(sp[0]), heads(sp[1]), heads(sp[2]), ck, cv, page_table + l * n_phys,
                               p["lam"], lam_init)
        r_s, st_s = retention_step(heads(sp[3]), heads(sp[4]), heads(sp[5]), state_ret[l])
        x1s, h2s, idxs, ws, ranks, cnts = merge_router(
            a_s.reshape(bd, A_WIDTH), r_s.reshape(bd, R_WIDTH), sp[6], xs, mod_s, p, lam_init, bd, True)
        xs = moe_block(x1s, h2s, idxs, ws, ranks, cnts, mod_s, p, bd, 1, True)
        outs[3].append(sp[1].reshape(bd, 1, A_HEADS, A_HEAD_DIM))
        outs[4].append(sp[2].reshape(bd, 1, A_HEADS, A_HEAD_DIM))
        outs[5].append(st_s)

    return (xp, xs.reshape(bd, 1, D_MODEL), jnp.stack(outs[0]), jnp.stack(outs[1]), jnp.stack(outs[2]),
            jnp.stack(outs[3]), jnp.stack(outs[4]), jnp.stack(outs[5]))
```
